```python
import jax, jax.numpy as jnp
from jax import lax
import numpy as np

D_MODEL = 1024
BATCH = 8
SEQ = 4096
DEPTH = 1

CHUNK = 64
N_MEM = 256
ROPE_THETA = 10000.0
EPS = 1e-6
ATTN_HEADS = 8
HEAD_DIM = 64
ATTN_WIDTH = ATTN_HEADS * HEAD_DIM
IDX_HEADS = 8
IDX_DIM = HEAD_DIM
MAX_TOPK = 256
Q_BLOCK = CHUNK
CONV_CH = D_MODEL // 2
CONV_WIDTH = 31
N_BRANCHES = 2
IN_SPLITS = (ATTN_WIDTH, ATTN_WIDTH, ATTN_WIDTH, IDX_HEADS * IDX_DIM, IDX_DIM, IDX_HEADS,
             2 * CONV_CH, N_BRANCHES * D_MODEL)
IN_WIDTH = sum(IN_SPLITS)
X_HEADS = 4
X_HEAD_DIM = D_MODEL // X_HEADS
N_GROUPS = 4
EXPERTS_PER_GROUP = 8
TOP_K_EXPERTS = 2
EXPERT_FF = D_MODEL // 4

kernel_name = "hybrid_dsa_conformer_hmoe_block"


def rmsnorm(x, g):
    xf = x.astype(jnp.float32)
    y = xf * lax.rsqrt(jnp.mean(xf * xf, axis=-1, keepdims=True) + EPS)
    return (y * g.astype(jnp.float32)).astype(x.dtype)


def layernorm(x, g, b):
    xf = x.astype(jnp.float32)
    mu = jnp.mean(xf, axis=-1, keepdims=True)
    var = jnp.mean(jnp.square(xf - mu), axis=-1, keepdims=True)
    y = (xf - mu) * lax.rsqrt(var + EPS)
    return (y * g.astype(jnp.float32) + b.astype(jnp.float32)).astype(x.dtype)


def rope_tables(positions, dim):
    inv_freq = ROPE_THETA ** (-jnp.arange(0, dim, 2, dtype=jnp.float32) / dim)
    ang = positions.astype(jnp.float32)[..., None] * inv_freq
    return jnp.cos(ang)[:, :, None, :], jnp.sin(ang)[:, :, None, :]


def apply_rope(x, cos, sin):
    xf = x.astype(jnp.float32)
    half = xf.shape[-1] // 2
    x1, x2 = xf[..., :half], xf[..., half:]
    return jnp.concatenate([x1 * cos - x2 * sin, x2 * cos + x1 * sin], axis=-1).astype(x.dtype)


def _split_offsets():
    offs, acc = [], 0
    for w in IN_SPLITS[:-1]:
        acc += w
        offs.append(acc)
    return offs


def dsa_attention(q, k, v, q_idx, k_idx, w_idx):
    B, S = q.shape[0], q.shape[1]
    top_k = min(MAX_TOPK, S // 4)
    n_blk = S // Q_BLOCK
    key_pos = jnp.arange(S, dtype=jnp.int32)
    scale_attn = HEAD_DIM ** -0.5
    scale_idx = (IDX_DIM ** -0.5) * (IDX_HEADS ** -0.5)
    gather = jax.vmap(lambda src, idx: src[idx])

    def to_blocks(a):
        return a.reshape(B, n_blk, Q_BLOCK, *a.shape[2:]).swapaxes(0, 1)

    def block(args):
        qb, qib, wb, start = args
        qpos = start + jnp.arange(Q_BLOCK, dtype=jnp.int32)
        limit = (qpos // CHUNK + 1) * CHUNK
        admissible = key_pos[None, :] < limit[:, None]
        rel = jax.nn.relu(jnp.einsum('bqhd,bsd->bqhs', qib, k_idx).astype(jnp.float32))
        score = jnp.einsum('bqhs,bqh->bqs', rel, wb.astype(jnp.float32)) * scale_idx
        score = jnp.where(admissible[None], score, -jnp.inf)
        _, sel = lax.top_k(score, top_k)
        valid = sel < limit[None, :, None]
        k_sel = gather(k, sel)
        v_sel = gather(v, sel)
        logits = jnp.einsum('bqhd,bqkhd->bqhk', qb, k_sel).astype(jnp.float32) * scale_attn
        logits = jnp.where(valid[:, :, None, :], logits, -jnp.inf)
        p = jax.nn.softmax(logits, axis=-1).astype(v.dtype)
        return jnp.einsum('bqhk,bqkhd->bqhd', p, v_sel)

    starts = jnp.arange(n_blk, dtype=jnp.int32) * Q_BLOCK
    out = lax.map(block, (to_blocks(q), to_blocks(q_idx), to_blocks(w_idx), starts))
    return out.swapaxes(0, 1).reshape(B, S, ATTN_WIDTH)


def conformer_conv(a, b, conv_w, conv_b, ln_g, ln_b, w_pw):
    u = a * jax.nn.sigmoid(b)
    u = lax.conv_general_dilated(u, conv_w[:, None, :], window_strides=(1,),
                                 padding=[(CONV_WIDTH - 1, 0)],
                                 dimension_numbers=('NWC', 'WIO', 'NWC'),
                                 feature_group_count=CONV_CH) + conv_b
    u = jax.nn.silu(layernorm(u, ln_g, ln_b))
    return u @ w_pw


def mixing_sublayer(x, cos, sin, norm_g, w_in, w_o_attn, conv_w, conv_b, conv_ln_g, conv_ln_b,
                    w_conv_out, w_out):
    B, S, _ = x.shape
    h = rmsnorm(x, norm_g)
    q, k, v, qi, ki, wi, glu, gates = jnp.split(h @ w_in, _split_offsets(), axis=-1)
    q = apply_rope(q.reshape(B, S, ATTN_HEADS, HEAD_DIM), cos, sin)
    k = apply_rope(k.reshape(B, S, ATTN_HEADS, HEAD_DIM), cos, sin)
    v = v.reshape(B, S, ATTN_HEADS, HEAD_DIM)
    qi = apply_rope(qi.reshape(B, S, IDX_HEADS, IDX_DIM), cos, sin)
    ki = apply_rope(ki[:, :, None, :], cos, sin)[:, :, 0]
    y_attn = dsa_attention(q, k, v, qi, ki, wi) @ w_o_attn
    y_conv = conformer_conv(glu[..., :CONV_CH], glu[..., CONV_CH:], conv_w, conv_b,
                            conv_ln_g, conv_ln_b, w_conv_out)
    g = jax.nn.sigmoid(gates).reshape(B, S, N_BRANCHES, D_MODEL)
    return x + (g[:, :, 0] * y_attn + g[:, :, 1] * y_conv) @ w_out


def memory_cross_attention(x, mem, norm_g, norm_mem_g, w_q, w_kv, w_o):
    B, S, _ = x.shape
    M = mem.shape[1]
    q = (rmsnorm(x, norm_g) @ w_q).reshape(B, S, X_HEADS, X_HEAD_DIM)
    kv = (rmsnorm(mem, norm_mem_g) @ w_kv).reshape(B, M, 2, X_HEADS, X_HEAD_DIM)
    k, v = kv[:, :, 0], kv[:, :, 1]
    logits = jnp.einsum('bshd,bmhd->bhsm', q, k).astype(jnp.float32) * (X_HEAD_DIM ** -0.5)
    p = jax.nn.softmax(logits, axis=-1).astype(v.dtype)
    o = jnp.einsum('bhsm,bmhd->bshd', p, v).reshape(B, S, D_MODEL)
    return x + o @ w_o


def hier_moe(h, w_rg, b_rg, w_re, b_re, w_gate, w_up, w_down):
    B, S, D = h.shape
    t = h.reshape(-1, D)
    group_logits = (t @ w_rg + b_rg).astype(jnp.float32)
    g_sel = jnp.argmax(group_logits, axis=-1)
    g_gate = jnp.take_along_axis(jax.nn.softmax(group_logits, axis=-1), g_sel[:, None], axis=-1)
    exp_logits = (jnp.einsum('td,gde->tge', t, w_re) + b_re).astype(jnp.float32)
    exp_logits = jnp.take_along_axis(exp_logits, g_sel[:, None, None], axis=1)[:, 0]
    top_val, top_idx = lax.top_k(exp_logits, TOP_K_EXPERTS)
    top_w = jax.nn.softmax(top_val, axis=-1) * g_gate
    w_exp = jnp.einsum('tk,tke->te', top_w,
                       jax.nn.one_hot(top_idx, EXPERTS_PER_GROUP, dtype=jnp.float32))
    combine = (jax.nn.one_hot(g_sel, N_GROUPS, dtype=jnp.float32)[:, :, None]
               * w_exp[:, None, :]).astype(h.dtype)
    y = jnp.zeros_like(t)
    for g in range(N_GROUPS):
        act = (jax.nn.silu(jnp.einsum('td,edf->tef', t, w_gate[g]))
               * jnp.einsum('td,edf->tef', t, w_up[g]))
        y = y + jnp.einsum('tef,efd->td', act * combine[:, g, :, None], w_down[g])
    return y.reshape(B, S, D)


def setup_inputs(seed: int = 0) -> dict:
    key = jax.random.key(seed)
    ks = iter(jax.random.split(key, 40))
    L, D, G, E, F = DEPTH, D_MODEL, N_GROUPS, EXPERTS_PER_GROUP, EXPERT_FF

    def nrm(shape, scale):
        return jax.random.normal(next(ks), shape, jnp.float32) * scale

    def gain(shape):
        return 1.0 + nrm(shape, 0.01)

    x = nrm((BATCH, SEQ, D), 1.0)
    mem = nrm((BATCH, N_MEM, D), 1.0)
    offset = jax.random.randint(next(ks), (BATCH, 1), 0, 64, dtype=jnp.int32) * CHUNK
    positions = (offset + jnp.arange(SEQ, dtype=jnp.int32)[None, :]).astype(jnp.int32)
    return {
        "x": x,
        "mem": mem,
        "positions": positions,
        "norm_mix_g": gain((L, D)),
        "w_in": nrm((L, D, IN_WIDTH), D ** -0.5),
        "w_o_attn": nrm((L, ATTN_WIDTH, D), ATTN_WIDTH ** -0.5),
        "conv_w": nrm((L, CONV_WIDTH, CONV_CH), CONV_WIDTH ** -0.5),
        "conv_b": nrm((L, CONV_CH), 0.01),
        "conv_ln_g": gain((L, CONV_CH)),
        "conv_ln_b": nrm((L, CONV_CH), 0.01),
        "w_conv_out": nrm((L, CONV_CH, D), CONV_CH ** -0.5),
        "w_out": nrm((L, D, D), D ** -0.5),
        "norm_x_g": gain((L, D)),
        "norm_mem_g": gain((L, D)),
        "w_q_x": nrm((L, D, D), D ** -0.5),
        "w_kv_x": nrm((L, D, 2 * D), D ** -0.5),
        "w_o_x": nrm((L, D, D), D ** -0.5),
        "norm_moe_g": gain((L, D)),
        "w_router_group": nrm((L, D, G), D ** -0.5),
        "b_router_group": nrm((L, G), 0.01),
        "w_router_expert": nrm((L, G, D, E), D ** -0.5),
        "b_router_expert": nrm((L, G, E), 0.01),
        "w_exp_gate": nrm((L, G, E, D, F), D ** -0.5),
        "w_exp_up": nrm((L, G, E, D, F), D ** -0.5),
        "w_exp_down": nrm((L, G, E, F, D), F ** -0.5),
        "norm_final_g": gain((D,)),
    }


def reference(x, mem, positions, norm_mix_g, w_in, w_o_attn, conv_w, conv_b, conv_ln_g, conv_ln_b,
              w_conv_out, w_out, norm_x_g, norm_mem_g, w_q_x, w_kv_x, w_o_x, norm_moe_g,
              w_router_group, b_router_group, w_router_expert, b_router_expert,
              w_exp_gate, w_exp_up, w_exp_down, norm_final_g):
    cos, sin = rope_tables(positions, HEAD_DIM)
    for l in range(DEPTH):
        x = mixing_sublayer(x, cos, sin, norm_mix_g[l], w_in[l], w_o_attn[l], conv_w[l], conv_b[l],
                            conv_ln_g[l], conv_ln_b[l], w_conv_out[l], w_out[l])
        x = memory_cross_attention(x, mem, norm_x_g[l], norm_mem_g[l], w_q_x[l], w_kv_x[l], w_o_x[l])
        x = x + hier_moe(rmsnorm(x, norm_moe_g[l]), w_router_group[l], b_router_group[l],
                         w_router_expert[l], b_router_expert[l],
                         w_exp_gate[l], w_exp_up[l], w_exp_down[l])
    return rmsnorm(x, norm_final_g)
```

```python
import functools

import jax
import jax.numpy as jnp
from jax import lax
from jax.experimental import pallas as pl
from jax.experimental.pallas import tpu as pltpu

F32 = jnp.float32
BF16 = jnp.bfloat16
I32 = jnp.int32

D_MODEL = 1024
CHUNK = 64
ROPE_THETA = 10000.0
EPS = 1e-6
ATTN_HEADS = 8
HEAD_DIM = 64
ATTN_WIDTH = ATTN_HEADS * HEAD_DIM
IDX_HEADS = 8
IDX_DIM = 64
MAX_TOPK = 256
CONV_CH = D_MODEL // 2
CONV_WIDTH = 31
X_HEADS = 4
X_HEAD_DIM = D_MODEL // X_HEADS
N_GROUPS = 4
EXPERTS_PER_GROUP = 8
N_EXPERTS = N_GROUPS * EXPERTS_PER_GROUP
EXPERT_FF = D_MODEL // 4

LANES = 128
VMEM_LIMIT = 56 * 1024 * 1024
NEG_BIG = -1e30

TM_IN = 256
QT = 256
KB = 128
SEARCH_FIXED_STEPS = 12
TC_CONV = 256
HALO = 32
TM_MIX = 256
TT_MOE = 1024


def _cparams(sem):
    return pltpu.CompilerParams(dimension_semantics=sem, vmem_limit_bytes=VMEM_LIMIT)


def _rms(x, g):
    return x * lax.rsqrt(jnp.mean(x * x, axis=-1, keepdims=True) + EPS) * g


def _inproj_kernel(x_ref, g_ref, wm_ref, ws_ref, cos_ref, sin_ref,
                   q_ref, k_ref, qi_ref, v_ref, sm_ref, u_ref, gs_ref):
    tm = x_ref.shape[0]
    h = _rms(x_ref[...], g_ref[...]).astype(BF16)
    cos = cos_ref[...]
    sin = sin_ref[...]
    lane = lax.broadcasted_iota(I32, (tm, LANES), 1)
    first_half = (lane % HEAD_DIM) < (HEAD_DIM // 2)

    def rope(y):
        rot = jnp.where(first_half, pltpu.roll(y, LANES - HEAD_DIM // 2, 1),
                        pltpu.roll(y, HEAD_DIM // 2, 1))
        return y * cos + rot * sin

    def proj(c0, w):
        return jnp.dot(h, wm_ref[:, c0:c0 + w], preferred_element_type=F32)

    for ref, c0, scale in ((q_ref, 0, HEAD_DIM ** -0.5), (k_ref, 512, None), (qi_ref, 1024, None)):
        y = proj(c0, ATTN_WIDTH)
        for c in range(ATTN_WIDTH // LANES):
            r = rope(y[:, c * LANES:(c + 1) * LANES])
            if scale is not None:
                r = r * scale
            ref[:, c * LANES:(c + 1) * LANES] = r.astype(BF16)
    v_ref[...] = proj(1536, ATTN_WIDTH).astype(BF16)
    glu = proj(2048, 2 * CONV_CH)
    u_ref[...] = glu[:, :CONV_CH] * jax.nn.sigmoid(glu[:, CONV_CH:])
    for c in range(4):
        gs_ref[:, c * 512:(c + 1) * 512] = jax.nn.sigmoid(proj(3072 + c * 512, 512))
    ys = jnp.dot(h, ws_ref[...], preferred_element_type=F32)
    sm_ref[...] = jnp.where(lane < IDX_DIM, rope(ys), ys)


def _in_proj(x2d, g, wm, ws, cos128, sin128):
    T = x2d.shape[0]
    tm = TM_IN
    row = lambda w: pl.BlockSpec((tm, w), lambda i: (i, 0))
    full = lambda a: pl.BlockSpec(a.shape, lambda i: (0,) * a.ndim)
    return pl.pallas_call(
        _inproj_kernel,
        grid=(T // tm,),
        in_specs=[row(D_MODEL), full(g), full(wm), full(ws), row(LANES), row(LANES)],
        out_specs=[row(512), row(512), row(512), row(512), row(LANES), row(CONV_CH), row(2 * D_MODEL)],
        out_shape=[jax.ShapeDtypeStruct((T, 512), BF16)] * 4
        + [jax.ShapeDtypeStruct((T, LANES), F32), jax.ShapeDtypeStruct((T, CONV_CH), F32),
           jax.ShapeDtypeStruct((T, 2 * D_MODEL), F32)],
        compiler_params=_cparams(("parallel",)),
        name="in_proj",
    )(x2d, g, wm, ws, cos128, sin128)


def _dsa_kernel(q_ref, qi_ref, wi_ref, ki_ref, k_ref, vt_ref, o_ref,
                sc_ref, bias_ref, acc_ref, sa_ref, sb_ref, pa_ref, pb_ref):
    assert QT == 2 * KB
    t = pl.program_id(1)
    n_qb = t + 1
    n_kb = 2 * n_qb
    scale_idx = (IDX_DIM ** -0.5) * (IDX_HEADS ** -0.5)
    nt_dims = (((1,), (1,)), ((), ()))
    top_k = float(MAX_TOPK)

    qcol = lax.broadcasted_iota(I32, (1, QT), 1) + t * QT
    limit = (qcol // CHUNK + 1) * CHUNK
    w_idx = wi_ref[0]

    def score_rows(ks, masked):
        ki_blk = ki_ref[0, pl.ds(ks, KB), :]
        acc = jnp.zeros((KB, QT), F32)
        for h in range(IDX_HEADS):
            rel = lax.dot_general(ki_blk, qi_ref[0, h], nt_dims, preferred_element_type=F32)
            acc = acc + jnp.maximum(rel, 0.0) * w_idx[h:h + 1, :]
        score = acc * scale_idx
        if masked:
            adm = lax.broadcasted_iota(I32, (KB, QT), 0) + ks < limit
            lo_src = jnp.where(adm, score, jnp.inf)
            score = jnp.where(adm, score, -jnp.inf)
        else:
            lo_src = score
        sc_ref[pl.ds(ks, KB), :] = score
        fold = lambda a, op: functools.reduce(op, [a[r * 8:(r + 1) * 8] for r in range(KB // 8)])
        return fold(lo_src, jnp.minimum), fold(score, jnp.maximum)

    def score_block(qb, carry, masked=False):
        mn, mx = carry
        for sub in range(QT // KB):
            ks = pl.multiple_of(qb * QT + sub * KB, KB)
            bmn, bmx = score_rows(ks, masked)
            mn, mx = jnp.minimum(mn, bmn), jnp.maximum(mx, bmx)
        return mn, mx

    init = (jnp.full((8, QT), jnp.inf, F32), jnp.full((8, QT), -jnp.inf, F32))
    mn8, mx8 = score_block(t, lax.fori_loop(0, t, score_block, init), masked=True)
    s_min = jnp.min(mn8, axis=0, keepdims=True)
    s_max = jnp.max(mx8, axis=0, keepdims=True)

    def count_ge(cand):
        def body(qb, c):
            qs = pl.multiple_of(qb * QT, QT)
            hit = jnp.where(sc_ref[pl.ds(qs, QT), :] >= cand, 1.0, 0.0)
            return c + jnp.sum(hit.reshape(QT // 8, 8, QT), axis=0)
        c = lax.fori_loop(0, n_qb, body, jnp.zeros((8, QT), F32))
        return jnp.sum(c, axis=0, keepdims=True)

    def search_step(c):
        lo, hi, c_lo, active = c
        mid = 0.5 * lo + 0.5 * hi
        cnt = count_ge(mid)
        up = (active > 0.0) & (cnt >= top_k)
        down = (active > 0.0) & (cnt < top_k)
        stuck = (mid <= lo) | (mid >= hi)
        lo, c_lo, hi = jnp.where(up, mid, lo), jnp.where(up, cnt, c_lo), jnp.where(down, mid, hi)
        active = jnp.where((c_lo > top_k) & jnp.logical_not(stuck), active, 0.0)
        return lo, hi, c_lo, active

    n_adm = limit.astype(F32)
    state = (s_min, s_max, n_adm, jnp.where(n_adm > top_k, 1.0, 0.0))
    state = lax.fori_loop(0, SEARCH_FIXED_STEPS, lambda i, c: search_step(c), state)
    lo, hi, c_lo, _, _ = lax.while_loop(
        lambda c: jnp.logical_and(jnp.max(c[3]) > 0.0, c[4] < 200),
        lambda c: search_step(search_step(c[:4])) + (c[4] + 1,),
        state + (jnp.int32(0),))

    def bias_block(qb, carry):
        qs = pl.multiple_of(qb * QT, QT)
        bias_ref[pl.ds(qs, QT), :] = jnp.where(sc_ref[pl.ds(qs, QT), :] >= lo, 0.0, NEG_BIG)
        return carry

    lax.fori_loop(0, n_qb, bias_block, 0)

    @pl.when(jnp.max(c_lo) > top_k)
    def _():
        tau = jnp.where(count_ge(hi) >= top_k, hi, lo)

        def count_gt(qb, c):
            qs = pl.multiple_of(qb * QT, QT)
            hit = jnp.where(sc_ref[pl.ds(qs, QT), :] > tau, 1.0, 0.0)
            return c + jnp.sum(hit, axis=0, keepdims=True)

        need = top_k - lax.fori_loop(0, n_qb, count_gt, jnp.zeros((1, QT), F32))
        r = lax.broadcasted_iota(I32, (QT, QT), 0)
        c = lax.broadcasted_iota(I32, (QT, QT), 1)
        lower = jnp.where(r > c, 1.0, 0.0).astype(BF16)

        def tie_block(qb, seen):
            qs = pl.multiple_of(qb * QT, QT)
            blk = sc_ref[pl.ds(qs, QT), :]
            eq = blk == tau
            eqf = jnp.where(eq, 1.0, 0.0)
            rank = jnp.dot(lower, eqf.astype(BF16), preferred_element_type=F32) + seen
            sel = (blk > tau) | (eq & (rank < need))
            bias_ref[pl.ds(qs, QT), :] = jnp.where(sel, 0.0, NEG_BIG)
            return seen + jnp.sum(eqf, axis=0, keepdims=True)

        lax.fori_loop(0, n_qb, tie_block, jnp.zeros((1, QT), F32))

    def logits(kb, s_ref):
        ks = pl.multiple_of(kb * KB, KB)
        bias = bias_ref[pl.ds(ks, KB), :]
        for h in range(ATTN_HEADS):
            s_ref[h] = lax.dot_general(k_ref[0, h, pl.ds(ks, KB), :], q_ref[0, h], nt_dims,
                                       preferred_element_type=F32) + bias

    def softmax(s_ref, p_ref, m, l):
        ms, ls, alphas = [], [], []
        for h in range(ATTN_HEADS):
            s = s_ref[h]
            m_old = m[h:h + 1]
            m_new = jnp.maximum(m_old, jnp.max(s, axis=0, keepdims=True))
            alpha = jnp.exp(m_old - m_new)
            p = jnp.exp(s - m_new)
            p_ref[h] = p.astype(BF16)
            ls.append(l[h:h + 1] * alpha + jnp.sum(p, axis=0, keepdims=True))
            ms.append(m_new)
            alphas.append(alpha)
        cat = lambda rows: jnp.concatenate(rows, axis=0)
        return cat(ms), cat(ls), cat(alphas)

    def values(kb, p_ref, alpha):
        for h in range(ATTN_HEADS):
            acc_ref[h] = acc_ref[h] * alpha[h:h + 1] + jnp.dot(
                vt_ref[0, h, kb], p_ref[h], preferred_element_type=F32)

    acc_ref[...] = jnp.zeros(acc_ref.shape, F32)
    pb_ref[...] = jnp.zeros(pb_ref.shape, BF16)
    logits(0, sa_ref)

    def attn_pair(j, carry):
        m, l, alpha_b = carry
        kb0 = 2 * j
        logits(kb0 + 1, sb_ref)
        m, l, alpha_a = softmax(sa_ref, pa_ref, m, l)
        values(jnp.maximum(kb0 - 1, 0), pb_ref, alpha_b)
        logits(jnp.minimum(kb0 + 2, n_kb - 1), sa_ref)
        m, l, alpha_b = softmax(sb_ref, pb_ref, m, l)
        values(kb0, pa_ref, alpha_a)
        return m, l, alpha_b

    heads_qt = (ATTN_HEADS, QT)
    m, l, alpha_b = lax.fori_loop(
        0, n_qb, attn_pair,
        (jnp.full(heads_qt, NEG_BIG, F32), jnp.zeros(heads_qt, F32), jnp.ones(heads_qt, F32)))
    values(n_kb - 1, pb_ref, alpha_b)
    for h in range(ATTN_HEADS):
        o_ref[0, h * HEAD_DIM:(h + 1) * HEAD_DIM, :] = acc_ref[h] / l[h:h + 1]


def _dsa(q_hm, qi_hm, wi_t, ki, k_hm, vt_blk):
    B, _, S, _ = q_hm.shape
    return pl.pallas_call(
        _dsa_kernel,
        grid=(B, S // QT),
        in_specs=[
            pl.BlockSpec((1, ATTN_HEADS, QT, HEAD_DIM), lambda b, t: (b, 0, t, 0)),
            pl.BlockSpec((1, IDX_HEADS, QT, IDX_DIM), lambda b, t: (b, 0, t, 0)),
            pl.BlockSpec((1, IDX_HEADS, QT), lambda b, t: (b, 0, t)),
            pl.BlockSpec((1, S, IDX_DIM), lambda b, t: (b, 0, 0)),
            pl.BlockSpec((1, ATTN_HEADS, S, HEAD_DIM), lambda b, t: (b, 0, 0, 0)),
            pl.BlockSpec((1, ATTN_HEADS, S // KB, HEAD_DIM, KB), lambda b, t: (b, 0, 0, 0, 0)),
        ],
        out_specs=pl.BlockSpec((1, ATTN_WIDTH, QT), lambda b, t: (b, 0, t)),
        out_shape=jax.ShapeDtypeStruct((B, ATTN_WIDTH, S), F32),
        scratch_shapes=[pltpu.VMEM((S, QT), F32), pltpu.VMEM((S, QT), F32),
                        pltpu.VMEM((ATTN_HEADS, HEAD_DIM, QT), F32),
                        pltpu.VMEM((ATTN_HEADS, KB, QT), F32), pltpu.VMEM((ATTN_HEADS, KB, QT), F32),
                        pltpu.VMEM((ATTN_HEADS, KB, QT), BF16), pltpu.VMEM((ATTN_HEADS, KB, QT), BF16)],
        compiler_params=_cparams(("parallel", "arbitrary")),
        name="dsa",
    )(q_hm, qi_hm, wi_t, ki, k_hm, vt_blk)


def _conv_kernel(u_ref, halo_ref, w_ref, b_ref, lg_ref, lb_ref, o_ref, buf_ref):
    i = pl.program_id(1)
    tc = u_ref.shape[1]
    halo = halo_ref[0]
    buf_ref[0:HALO, :] = jnp.where(i > 0, halo, jnp.zeros_like(halo))
    buf_ref[HALO:HALO + tc, :] = u_ref[0]
    rows = 64
    off = HALO - (CONV_WIDTH - 1)
    for r in range(tc // rows):
        acc = jnp.zeros((rows, CONV_CH), F32) + b_ref[...]
        for j in range(CONV_WIDTH):
            s = r * rows + off + j
            acc = acc + buf_ref[s:s + rows, :] * w_ref[j:j + 1, :]
        mu = jnp.mean(acc, axis=-1, keepdims=True)
        d = acc - mu
        var = jnp.mean(d * d, axis=-1, keepdims=True)
        y = d * lax.rsqrt(var + EPS) * lg_ref[...] + lb_ref[...]
        o_ref[0, r * rows:(r + 1) * rows, :] = (y * jax.nn.sigmoid(y)).astype(BF16)


def _conv(u, conv_w, conv_b, ln_g, ln_b):
    B, S, C = u.shape
    tc = TC_CONV
    full = lambda a: pl.BlockSpec(a.shape, lambda b, i: (0,) * a.ndim)
    return pl.pallas_call(
        _conv_kernel,
        grid=(B, S // tc),
        in_specs=[
            pl.BlockSpec((1, tc, C), lambda b, i: (b, i, 0)),
            pl.BlockSpec((1, HALO, C), lambda b, i: (b, jnp.maximum(i * (tc // HALO) - 1, 0), 0)),
            full(conv_w), full(conv_b), full(ln_g), full(ln_b),
        ],
        out_specs=pl.BlockSpec((1, tc, C), lambda b, i: (b, i, 0)),
        out_shape=jax.ShapeDtypeStruct((B, S, C), BF16),
        scratch_shapes=[pltpu.VMEM((HALO + tc, C), F32)],
        compiler_params=_cparams(("parallel", "parallel")),
        name="conv",
    )(u, u, conv_w, conv_b, ln_g, ln_b)


def _memkv_kernel(mem_ref, g_ref, w_ref, o_ref):
    h = _rms(mem_ref[0], g_ref[...]).astype(BF16)
    o_ref[0] = jnp.dot(h, w_ref[...], preferred_element_type=F32).astype(BF16)


def _mem_kv(mem, g, w_kv):
    B, M, D = mem.shape
    return pl.pallas_call(
        _memkv_kernel,
        grid=(B,),
        in_specs=[pl.BlockSpec((1, M, D), lambda b: (b, 0, 0)),
                  pl.BlockSpec(g.shape, lambda b: (0, 0)),
                  pl.BlockSpec(w_kv.shape, lambda b: (0, 0))],
        out_specs=pl.BlockSpec((1, M, 2 * D), lambda b: (b, 0, 0)),
        out_shape=jax.ShapeDtypeStruct((B, M, 2 * D), BF16),
        compiler_params=_cparams(("parallel",)),
        name="mem_kv",
    )(mem, g, w_kv)


def _mix_kernel(x_ref, a_ref, c_ref, gs_ref, kv_ref, woa_ref, wco_ref, wout_ref, gx_ref,
                wq_ref, wox_ref, o_ref):
    dot = functools.partial(jnp.dot, preferred_element_type=F32)
    y_attn = dot(a_ref[0], woa_ref[...])
    y_conv = dot(c_ref[0], wco_ref[...])
    gs = gs_ref[0]
    merged = gs[:, :D_MODEL] * y_attn + gs[:, D_MODEL:] * y_conv
    x1 = x_ref[0] + dot(merged.astype(BF16), wout_ref[...])

    q = (dot(_rms(x1, gx_ref[...]).astype(BF16), wq_ref[...]) * (X_HEAD_DIM ** -0.5)).astype(BF16)
    heads = []
    for h in range(X_HEADS):
        lo = h * X_HEAD_DIM
        kh = kv_ref[0, :, lo:lo + X_HEAD_DIM]
        vh = kv_ref[0, :, D_MODEL + lo:D_MODEL + lo + X_HEAD_DIM]
        s = lax.dot_general(q[:, lo:lo + X_HEAD_DIM], kh, (((1,), (1,)), ((), ())),
                            preferred_element_type=F32)
        p = jnp.exp(s - jnp.max(s, axis=-1, keepdims=True))
        p = p / jnp.sum(p, axis=-1, keepdims=True)
        heads.append(dot(p.astype(BF16), vh))
    o = jnp.concatenate(heads, axis=-1)
    o_ref[0] = x1 + dot(o.astype(BF16), wox_ref[...])


def _mix_xattn(x, attn, uc, gs, kv, woa, wco, wout, gx, wq, wox):
    B, S, D = x.shape
    tm = TM_MIX
    row = lambda w: pl.BlockSpec((1, tm, w), lambda b, i: (b, i, 0))
    full = lambda a: pl.BlockSpec(a.shape, lambda b, i: (0,) * a.ndim)
    return pl.pallas_call(
        _mix_kernel,
        grid=(B, S // tm),
        in_specs=[row(D), row(ATTN_WIDTH), row(CONV_CH), row(2 * D),
                  pl.BlockSpec((1,) + kv.shape[1:], lambda b, i: (b, 0, 0)),
                  full(woa), full(wco), full(wout), full(gx), full(wq), full(wox)],
        out_specs=row(D),
        out_shape=jax.ShapeDtypeStruct((B, S, D), F32),
        compiler_params=_cparams(("parallel", "parallel")),
        name="mix_xattn",
    )(x, attn, uc, gs, kv, woa, wco, wout, gx, wq, wox)


def _moe_kernel(x_ref, gm_ref, wr_ref, br_ref, wg_ref, wu_ref, wd_ref, gf_ref, o_ref,
                hn_ref, comb_ref, y_ref):
    g = pl.program_id(1)
    tt = x_ref.shape[0]
    lane = lax.broadcasted_iota(I32, (tt, LANES), 1)

    @pl.when(g == 0)
    def _():
        hn = _rms(x_ref[...], gm_ref[...]).astype(BF16)
        hn_ref[...] = hn
        logits = jnp.dot(hn, wr_ref[...], preferred_element_type=F32) + br_ref[...]
        gl = jnp.where(lane < N_GROUPS, logits, -jnp.inf)
        gmax = jnp.max(gl, axis=-1, keepdims=True)
        g_sel = jnp.min(jnp.where(gl == gmax, lane, LANES), axis=-1, keepdims=True)
        g_gate = 1.0 / jnp.sum(jnp.exp(gl - gmax), axis=-1, keepdims=True)
        e_lo = N_GROUPS + g_sel * EXPERTS_PER_GROUP
        el = jnp.where((lane >= e_lo) & (lane < e_lo + EXPERTS_PER_GROUP), logits, -jnp.inf)
        top1 = jnp.max(el, axis=-1, keepdims=True)
        i1 = jnp.min(jnp.where(el == top1, lane, LANES), axis=-1, keepdims=True)
        el2 = jnp.where(lane == i1, -jnp.inf, el)
        top2 = jnp.max(el2, axis=-1, keepdims=True)
        i2 = jnp.min(jnp.where(el2 == top2, lane, LANES), axis=-1, keepdims=True)
        e2 = jnp.exp(top2 - top1)
        w1 = g_gate / (1.0 + e2)
        w2 = g_gate * e2 / (1.0 + e2)
        comb_ref[...] = jnp.where(lane == i1, w1, 0.0) + jnp.where(lane == i2, w2, 0.0)
        y_ref[...] = jnp.zeros_like(y_ref)

    hn = hn_ref[...]
    comb = comb_ref[...]
    y = y_ref[...]
    for e in range(EXPERTS_PER_GROUP):
        col = N_GROUPS + g * EXPERTS_PER_GROUP + e
        c = jnp.sum(jnp.where(lane == col, comb, 0.0), axis=-1, keepdims=True)
        gate = jnp.dot(hn, wg_ref[0, e], preferred_element_type=F32)
        up = jnp.dot(hn, wu_ref[0, e], preferred_element_type=F32)
        act = (gate * jax.nn.sigmoid(gate)) * up * c
        y = y + jnp.dot(act.astype(BF16), wd_ref[0, e], preferred_element_type=F32)
    y_ref[...] = y

    @pl.when(g == N_GROUPS - 1)
    def _():
        o_ref[...] = _rms(x_ref[...] + y_ref[...], gf_ref[...])


def _moe(x2d, gm, wr, br, wg, wu, wd, gf):
    T, D = x2d.shape
    tt = TT_MOE
    full = lambda a: pl.BlockSpec(a.shape, lambda i, g: (0,) * a.ndim)
    return pl.pallas_call(
        _moe_kernel,
        grid=(T // tt, N_GROUPS),
        in_specs=[pl.BlockSpec((tt, D), lambda i, g: (i, 0)), full(gm), full(wr), full(br),
                  pl.BlockSpec((1, EXPERTS_PER_GROUP, D, EXPERT_FF), lambda i, g: (g, 0, 0, 0)),
                  pl.BlockSpec((1, EXPERTS_PER_GROUP, D, EXPERT_FF), lambda i, g: (g, 0, 0, 0)),
                  pl.BlockSpec((1, EXPERTS_PER_GROUP, EXPERT_FF, D), lambda i, g: (g, 0, 0, 0)),
                  full(gf)],
        out_specs=pl.BlockSpec((tt, D), lambda i, g: (i, 0)),
        out_shape=jax.ShapeDtypeStruct((T, D), F32),
        scratch_shapes=[pltpu.VMEM((tt, D), BF16), pltpu.VMEM((tt, LANES), F32),
                        pltpu.VMEM((tt, D), F32)],
        compiler_params=_cparams(("parallel", "arbitrary")),
        name="moe",
    )(x2d, gm, wr, br, wg, wu, wd, gf)


def kernel(x, mem, positions, norm_mix_g, w_in, w_o_attn, conv_w, conv_b, conv_ln_g, conv_ln_b,
           w_conv_out, w_out, norm_x_g, norm_mem_g, w_q_x, w_kv_x, w_o_x, norm_moe_g,
           w_router_group, b_router_group, w_router_expert, b_router_expert,
           w_exp_gate, w_exp_up, w_exp_down, norm_final_g):
    B, S, D = x.shape
    T = B * S
    depth = norm_mix_g.shape[0]

    inv_freq = ROPE_THETA ** (-jnp.arange(0, HEAD_DIM, 2, dtype=F32) / HEAD_DIM)
    ang = positions.astype(F32)[..., None] * inv_freq
    cos, sin = jnp.cos(ang), jnp.sin(ang)
    cos128 = jnp.tile(cos, (1, 1, 4)).reshape(T, LANES)
    sin128 = jnp.tile(jnp.concatenate([-sin, sin], axis=-1), (1, 1, 2)).reshape(T, LANES)

    row = lambda a: a.reshape(1, -1)
    for l in range(depth):
        w = w_in[l]
        wm = jnp.concatenate([w[:, 0:512], w[:, 512:1024], w[:, 1536:2048], w[:, 1024:1536],
                              w[:, 2120:3144], w[:, 3144:5192]], axis=1).astype(BF16)
        ws = jnp.pad(w[:, 2048:2120], ((0, 0), (0, LANES - 72))).astype(BF16)
        q, k, qi, v, small, u, gs = _in_proj(x.reshape(T, D), row(norm_mix_g[l]), wm, ws, cos128, sin128)

        head_major = lambda a: a.reshape(B, S, ATTN_HEADS, HEAD_DIM).transpose(0, 2, 1, 3)
        vt_blk = v.reshape(B, S // KB, KB, ATTN_HEADS, HEAD_DIM).transpose(0, 3, 1, 4, 2)
        ki = small[:, :IDX_DIM].astype(BF16).reshape(B, S, IDX_DIM)
        wi_t = small[:, IDX_DIM:IDX_DIM + IDX_HEADS].reshape(B, S, IDX_HEADS).transpose(0, 2, 1)
        attn_t = _dsa(head_major(q), head_major(qi), wi_t, ki, head_major(k), vt_blk)
        attn = attn_t.transpose(0, 2, 1).astype(BF16)

        uc = _conv(u.reshape(B, S, CONV_CH), jnp.pad(conv_w[l], ((0, 1), (0, 0))), row(conv_b[l]),
                   row(conv_ln_g[l]), row(conv_ln_b[l]))
        kv = _mem_kv(mem, row(norm_mem_g[l]), w_kv_x[l].astype(BF16))
        x = _mix_xattn(x, attn, uc, gs.reshape(B, S, 2 * D), kv,
                       w_o_attn[l].astype(BF16), w_conv_out[l].astype(BF16), w_out[l].astype(BF16),
                       row(norm_x_g[l]), w_q_x[l].astype(BF16), w_o_x[l].astype(BF16))

        wr = jnp.concatenate([w_router_group[l],
                              w_router_expert[l].transpose(1, 0, 2).reshape(D, N_EXPERTS)], axis=1)
        wr = jnp.pad(wr, ((0, 0), (0, LANES - N_GROUPS - N_EXPERTS))).astype(BF16)
        br = jnp.pad(jnp.concatenate([b_router_group[l], b_router_expert[l].reshape(-1)]),
                     (0, LANES - N_GROUPS - N_EXPERTS)).reshape(1, LANES)
        last = l == depth - 1
        gf = row(norm_final_g) if last else None
        assert last, "the final norm is fused into the last layer's MoE kernel"
        x = _moe(x.reshape(T, D), row(norm_moe_g[l]), wr, br, w_exp_gate[l].astype(BF16),
                 w_exp_up[l].astype(BF16), w_exp_down[l].astype(BF16), gf).reshape(B, S, D)
    return x
```

```python
import functools

import jax
import jax.numpy as jnp
from jax import lax
from jax.experimental import pallas as pl
from jax.experimental.pallas import tpu as pltpu

F32 = jnp.float32
BF16 = jnp.bfloat16
I32 = jnp.int32

D_MODEL = 1024
CHUNK = 64
ROPE_THETA = 10000.0
EPS = 1e-6
ATTN_HEADS = 8
HEAD_DIM = 64
ATTN_WIDTH = ATTN_HEADS * HEAD_DIM
IDX_HEADS = 8
IDX_DIM = 64
MAX_TOPK = 256
CONV_CH = D_MODEL // 2
CONV_WIDTH = 31
X_HEADS = 4
X_HEAD_DIM = D_MODEL // X_HEADS
N_GROUPS = 4
EXPERTS_PER_GROUP = 8
N_EXPERTS = N_GROUPS * EXPERTS_PER_GROUP
EXPERT_FF = D_MODEL // 4

LANES = 128
SUBLANES = 8
VMEM_LIMIT = 56 * 1024 * 1024
NEG_BIG = -1e30

TM_IN = 256
QT = 256
KB = 128
SEARCH_FIXED_STEPS = 12
TC_CONV = 256
HALO = 32
TM_MIX = 256
TT_MOE = 1024


def _cparams(sem):
    return pltpu.CompilerParams(dimension_semantics=sem, vmem_limit_bytes=VMEM_LIMIT)


def _rms(x, g):
    return x * lax.rsqrt(jnp.mean(x * x, axis=-1, keepdims=True) + EPS) * g


def _inproj_kernel(x_ref, g_ref, wm_ref, ws_ref, wvt_ref, wwt_ref, cos_ref, sin_ref,
                   q_ref, k_ref, qi_ref, vt_ref, ki_ref, wit_ref, u_ref, gs_ref):
    tm = x_ref.shape[0]
    nt_dims = (((1,), (1,)), ((), ()))
    h = _rms(x_ref[...], g_ref[...]).astype(BF16)
    cos = cos_ref[...]
    sin = sin_ref[...]
    lane = lax.broadcasted_iota(I32, (tm, LANES), 1)
    first_half = (lane % HEAD_DIM) < (HEAD_DIM // 2)

    def rope(y):
        rot = jnp.where(first_half, pltpu.roll(y, LANES - HEAD_DIM // 2, 1),
                        pltpu.roll(y, HEAD_DIM // 2, 1))
        return y * cos + rot * sin

    def proj(c0, w):
        return jnp.dot(h, wm_ref[:, c0:c0 + w], preferred_element_type=F32)

    for ref, c0, scale in ((q_ref, 0, HEAD_DIM ** -0.5), (k_ref, 512, None), (qi_ref, 1024, None)):
        y = proj(c0, ATTN_WIDTH)
        for c in range(ATTN_WIDTH // LANES):
            r = rope(y[:, c * LANES:(c + 1) * LANES])
            if scale is not None:
                r = r * scale
            ref[:, c * LANES:(c + 1) * LANES] = r.astype(BF16)
    v_t = lax.dot_general(wvt_ref[...], h, nt_dims, preferred_element_type=F32)
    for c in range(tm // KB):
        vt_ref[c] = v_t[:, c * KB:(c + 1) * KB].astype(BF16)
    wit_ref[...] = lax.dot_general(wwt_ref[...], h, nt_dims, preferred_element_type=F32)
    glu = proj(1536, 2 * CONV_CH)
    u_ref[...] = glu[:, :CONV_CH] * jax.nn.sigmoid(glu[:, CONV_CH:])
    for c in range(4):
        gs_ref[:, c * 512:(c + 1) * 512] = jax.nn.sigmoid(proj(2560 + c * 512, 512))
    ys = jnp.dot(h, ws_ref[...], preferred_element_type=F32)
    ki_ref[...] = rope(ys)[:, :IDX_DIM].astype(BF16)


def _in_proj(x2d, g, wm, ws, wvt, wwt, cos128, sin128):
    T = x2d.shape[0]
    tm = TM_IN
    row = lambda w: pl.BlockSpec((tm, w), lambda i: (i, 0))
    full = lambda a: pl.BlockSpec(a.shape, lambda i: (0,) * a.ndim)
    return pl.pallas_call(
        _inproj_kernel,
        grid=(T // tm,),
        in_specs=[row(D_MODEL), full(g), full(wm), full(ws), full(wvt), full(wwt), row(LANES), row(LANES)],
        out_specs=[row(512), row(512), row(512),
                   pl.BlockSpec((tm // KB, ATTN_WIDTH, KB), lambda i: (i, 0, 0)),
                   row(IDX_DIM), pl.BlockSpec((IDX_HEADS, tm), lambda i: (0, i)),
                   row(CONV_CH), row(2 * D_MODEL)],
        out_shape=[jax.ShapeDtypeStruct((T, 512), BF16)] * 3
        + [jax.ShapeDtypeStruct((T // KB, ATTN_WIDTH, KB), BF16),
           jax.ShapeDtypeStruct((T, IDX_DIM), BF16), jax.ShapeDtypeStruct((IDX_HEADS, T), F32),
           jax.ShapeDtypeStruct((T, CONV_CH), F32), jax.ShapeDtypeStruct((T, 2 * D_MODEL), F32)],
        compiler_params=_cparams(("parallel",)),
        name="in_proj",
    )(x2d, g, wm, ws, wvt, wwt, cos128, sin128)


def _dsa_kernel(q_ref, qi_ref, wi_ref, ki_ref, k_ref, vt_ref, o_ref,
                sc_ref, bias_ref, acc_ref, sa_ref, sb_ref, pa_ref, pb_ref, qh_ref, qih_ref):
    assert QT == 2 * KB
    t = pl.program_id(1)
    n_qb = t + 1
    n_kb = 2 * n_qb
    scale_idx = (IDX_DIM ** -0.5) * (IDX_HEADS ** -0.5)
    nt_dims = (((1,), (1,)), ((), ()))
    top_k = float(MAX_TOPK)
    head = lambda h: slice(h * HEAD_DIM, (h + 1) * HEAD_DIM)

    qcol = lax.broadcasted_iota(I32, (1, QT), 1) + t * QT
    limit = (qcol // CHUNK + 1) * CHUNK
    w_idx = wi_ref[...]
    for h in range(ATTN_HEADS):
        qh_ref[h] = q_ref[:, head(h)]
        qih_ref[h] = qi_ref[:, head(h)]

    def score_rows(ks, masked):
        ki_blk = ki_ref[pl.ds(ks, KB), :]
        acc = jnp.zeros((KB, QT), F32)
        for h in range(IDX_HEADS):
            rel = lax.dot_general(ki_blk, qih_ref[h], nt_dims, preferred_element_type=F32)
            acc = acc + jnp.maximum(rel, 0.0) * w_idx[h:h + 1, :]
        score = acc * scale_idx
        if masked:
            adm = lax.broadcasted_iota(I32, (KB, QT), 0) + ks < limit
            lo_src = jnp.where(adm, score, jnp.inf)
            score = jnp.where(adm, score, -jnp.inf)
        else:
            lo_src = score
        sc_ref[pl.ds(ks, KB), :] = score
        fold = lambda a, op: functools.reduce(op, [a[r * 8:(r + 1) * 8] for r in range(KB // 8)])
        return fold(lo_src, jnp.minimum), fold(score, jnp.maximum)

    def score_block(qb, carry, masked=False):
        mn, mx = carry
        for sub in range(QT // KB):
            ks = pl.multiple_of(qb * QT + sub * KB, KB)
            bmn, bmx = score_rows(ks, masked)
            mn, mx = jnp.minimum(mn, bmn), jnp.maximum(mx, bmx)
        return mn, mx

    init = (jnp.full((8, QT), jnp.inf, F32), jnp.full((8, QT), -jnp.inf, F32))
    mn8, mx8 = score_block(t, lax.fori_loop(0, t, score_block, init), masked=True)
    s_min = jnp.min(mn8, axis=0, keepdims=True)
    s_max = jnp.max(mx8, axis=0, keepdims=True)

    def count(cand, strict=False):
        def body(qb, c):
            qs = pl.multiple_of(qb * QT, QT)
            blk = sc_ref[pl.ds(qs, QT), :]
            hit = jnp.where(blk > cand if strict else blk >= cand, 1.0, 0.0)
            return c + jnp.sum(hit.reshape(QT // 8, 8, QT), axis=0)
        c = lax.fori_loop(0, n_qb, body, jnp.zeros((8, QT), F32))
        return jnp.sum(c, axis=0, keepdims=True)

    count_ge = count

    def search_step(c):
        lo, hi, c_lo, active = c
        mid = 0.5 * lo + 0.5 * hi
        cnt = count_ge(mid)
        up = (active > 0.0) & (cnt >= top_k)
        down = (active > 0.0) & (cnt < top_k)
        stuck = (mid <= lo) | (mid >= hi)
        lo, c_lo, hi = jnp.where(up, mid, lo), jnp.where(up, cnt, c_lo), jnp.where(down, mid, hi)
        active = jnp.where((c_lo > top_k) & jnp.logical_not(stuck), active, 0.0)
        return lo, hi, c_lo, active

    n_adm = limit.astype(F32)
    searching = n_adm > top_k
    zero = jnp.zeros((1, QT), F32)
    c_ge0, c_gt0 = count(zero), count(zero, strict=True)
    above = searching & (c_gt0 >= top_k)
    at_zero = searching & (c_gt0 < top_k) & (c_ge0 >= top_k)
    below = searching & (c_ge0 < top_k)
    state = (jnp.where(above | at_zero, zero, s_min), jnp.where(below | at_zero, zero, s_max),
             jnp.where(above | at_zero, c_ge0, n_adm),
             jnp.where(above | below, 1.0, 0.0))
    state = lax.fori_loop(0, SEARCH_FIXED_STEPS, lambda i, c: search_step(c), state)
    lo, hi, c_lo, _, _ = lax.while_loop(
        lambda c: jnp.logical_and(jnp.max(c[3]) > 0.0, c[4] < 200),
        lambda c: search_step(search_step(c[:4])) + (c[4] + 1,),
        state + (jnp.int32(0),))

    def bias_block(qb, carry):
        qs = pl.multiple_of(qb * QT, QT)
        bias_ref[pl.ds(qs, QT), :] = jnp.where(sc_ref[pl.ds(qs, QT), :] >= lo, 0.0, NEG_BIG)
        return carry

    lax.fori_loop(0, n_qb, bias_block, 0)

    @pl.when(jnp.max(c_lo) > top_k)
    def _():
        tau = jnp.where(count_ge(hi) >= top_k, hi, lo)
        need = top_k - count(tau, strict=True)
        r = lax.broadcasted_iota(I32, (QT, QT), 0)
        c = lax.broadcasted_iota(I32, (QT, QT), 1)
        lower = jnp.where(r > c, 1.0, 0.0).astype(BF16)

        def tie_block(qb, seen):
            qs = pl.multiple_of(qb * QT, QT)
            blk = sc_ref[pl.ds(qs, QT), :]
            eq = blk == tau
            eqf = jnp.where(eq, 1.0, 0.0)
            rank = jnp.dot(lower, eqf.astype(BF16), preferred_element_type=F32) + seen
            sel = (blk > tau) | (eq & (rank < need))
            bias_ref[pl.ds(qs, QT), :] = jnp.where(sel, 0.0, NEG_BIG)
            return seen + jnp.sum(eqf, axis=0, keepdims=True)

        lax.fori_loop(0, n_qb, tie_block, jnp.zeros((1, QT), F32))

    def logits(kb, s_ref):
        ks = pl.multiple_of(kb * KB, KB)
        bias = bias_ref[pl.ds(ks, KB), :]
        for h in range(ATTN_HEADS):
            s_ref[h] = lax.dot_general(k_ref[pl.ds(ks, KB), head(h)], qh_ref[h], nt_dims,
                                       preferred_element_type=F32) + bias

    def softmax(s_ref, p_ref, m, l):
        ms, ls, alphas = [], [], []
        for h in range(ATTN_HEADS):
            s = s_ref[h]
            m_old = m[h:h + 1]
            m_new = jnp.maximum(m_old, jnp.max(s, axis=0, keepdims=True))
            alpha = jnp.exp(m_old - m_new)
            p = jnp.exp(s - m_new)
            p_ref[h] = p.astype(BF16)
            ls.append(l[h:h + 1] * alpha + jnp.sum(p, axis=0, keepdims=True))
            ms.append(m_new)
            alphas.append(alpha)
        cat = lambda rows: jnp.concatenate(rows, axis=0)
        return cat(ms), cat(ls), cat(alphas)

    def values(kb, p_ref, alpha):
        for h in range(ATTN_HEADS):
            acc_ref[h] = acc_ref[h] * alpha[h:h + 1] + jnp.dot(
                vt_ref[kb, head(h), :], p_ref[h], preferred_element_type=F32)

    acc_ref[...] = jnp.zeros(acc_ref.shape, F32)
    pb_ref[...] = jnp.zeros(pb_ref.shape, BF16)
    logits(0, sa_ref)

    def attn_pair(j, carry):
        m, l, alpha_b = carry
        kb0 = 2 * j
        logits(kb0 + 1, sb_ref)
        m, l, alpha_a = softmax(sa_ref, pa_ref, m, l)
        values(jnp.maximum(kb0 - 1, 0), pb_ref, alpha_b)
        logits(jnp.minimum(kb0 + 2, n_kb - 1), sa_ref)
        m, l, alpha_b = softmax(sb_ref, pb_ref, m, l)
        values(kb0, pa_ref, alpha_a)
        return m, l, alpha_b

    heads_qt = (ATTN_HEADS, QT)
    m, l, alpha_b = lax.fori_loop(
        0, n_qb, attn_pair,
        (jnp.full(heads_qt, NEG_BIG, F32), jnp.zeros(heads_qt, F32), jnp.ones(heads_qt, F32)))
    values(n_kb - 1, pb_ref, alpha_b)
    for j in range(ATTN_HEADS // 2):
        pair = jnp.concatenate([acc_ref[h] / l[h:h + 1] for h in (2 * j, 2 * j + 1)], axis=0)
        o_ref[:, j * LANES:(j + 1) * LANES] = pair.T.astype(BF16)


def _dsa(q, qi, wi_t, ki, k, vt_blk, B, S):
    nq = S // QT
    return pl.pallas_call(
        _dsa_kernel,
        grid=(B, nq),
        in_specs=[
            pl.BlockSpec((QT, ATTN_WIDTH), lambda b, t: (b * nq + t, 0)),
            pl.BlockSpec((QT, IDX_HEADS * IDX_DIM), lambda b, t: (b * nq + t, 0)),
            pl.BlockSpec((IDX_HEADS, QT), lambda b, t: (0, b * nq + t)),
            pl.BlockSpec((S, IDX_DIM), lambda b, t: (b, 0)),
            pl.BlockSpec((S, ATTN_WIDTH), lambda b, t: (b, 0)),
            pl.BlockSpec((S // KB, ATTN_WIDTH, KB), lambda b, t: (b, 0, 0)),
        ],
        out_specs=pl.BlockSpec((QT, ATTN_WIDTH), lambda b, t: (b * nq + t, 0)),
        out_shape=jax.ShapeDtypeStruct((B * S, ATTN_WIDTH), BF16),
        scratch_shapes=[pltpu.VMEM((S, QT), F32), pltpu.VMEM((S, QT), F32),
                        pltpu.VMEM((ATTN_HEADS, HEAD_DIM, QT), F32),
                        pltpu.VMEM((ATTN_HEADS, KB, QT), F32), pltpu.VMEM((ATTN_HEADS, KB, QT), F32),
                        pltpu.VMEM((ATTN_HEADS, KB, QT), BF16), pltpu.VMEM((ATTN_HEADS, KB, QT), BF16),
                        pltpu.VMEM((ATTN_HEADS, QT, HEAD_DIM), BF16),
                        pltpu.VMEM((IDX_HEADS, QT, IDX_DIM), BF16)],
        compiler_params=_cparams(("parallel", "arbitrary")),
        name="dsa",
    )(q, qi, wi_t, ki, k, vt_blk)


def _conv_kernel(u_ref, halo_ref, w_ref, b_ref, lg_ref, lb_ref, o_ref, buf_ref):
    i = pl.program_id(1)
    tc = u_ref.shape[1]
    halo = halo_ref[0]
    buf_ref[0, 0:HALO, :] = jnp.where(i > 0, halo, jnp.zeros_like(halo))
    buf_ref[0, HALO:HALO + tc, :] = u_ref[0]
    span = HALO + tc - SUBLANES
    for r in range(1, SUBLANES):
        buf_ref[r, 0:span, :] = buf_ref[0, r:r + span, :]
    off = HALO - (CONV_WIDTH - 1)
    rows = 64
    for rb in range(tc // rows):
        acc = jnp.zeros((rows, CONV_CH), F32) + b_ref[...]
        for j in range(CONV_WIDTH):
            r = (off + j) % SUBLANES
            s = rb * rows + off + j - r
            acc = acc + buf_ref[r, s:s + rows, :] * w_ref[j:j + 1, :]
        mu = jnp.mean(acc, axis=-1, keepdims=True)
        d = acc - mu
        var = jnp.mean(d * d, axis=-1, keepdims=True)
        y = d * lax.rsqrt(var + EPS) * lg_ref[...] + lb_ref[...]
        o_ref[0, rb * rows:(rb + 1) * rows, :] = (y * jax.nn.sigmoid(y)).astype(BF16)


def _conv(u, conv_w, conv_b, ln_g, ln_b):
    B, S, C = u.shape
    tc = TC_CONV
    full = lambda a: pl.BlockSpec(a.shape, lambda b, i: (0,) * a.ndim)
    return pl.pallas_call(
        _conv_kernel,
        grid=(B, S // tc),
        in_specs=[
            pl.BlockSpec((1, tc, C), lambda b, i: (b, i, 0)),
            pl.BlockSpec((1, HALO, C), lambda b, i: (b, jnp.maximum(i * (tc // HALO) - 1, 0), 0)),
            full(conv_w), full(conv_b), full(ln_g), full(ln_b),
        ],
        out_specs=pl.BlockSpec((1, tc, C), lambda b, i: (b, i, 0)),
        out_shape=jax.ShapeDtypeStruct((B, S, C), BF16),
        scratch_shapes=[pltpu.VMEM((SUBLANES, HALO + tc, C), F32)],
        compiler_params=_cparams(("parallel", "parallel")),
        name="conv",
    )(u, u, conv_w, conv_b, ln_g, ln_b)


def _memkv_kernel(mem_ref, g_ref, w_ref, o_ref):
    h = _rms(mem_ref[0], g_ref[...]).astype(BF16)
    o_ref[0] = jnp.dot(h, w_ref[...], preferred_element_type=F32).astype(BF16)


def _mem_kv(mem, g, w_kv):
    B, M, D = mem.shape
    return pl.pallas_call(
        _memkv_kernel,
        grid=(B,),
        in_specs=[pl.BlockSpec((1, M, D), lambda b: (b, 0, 0)),
                  pl.BlockSpec(g.shape, lambda b: (0, 0)),
                  pl.BlockSpec(w_kv.shape, lambda b: (0, 0))],
        out_specs=pl.BlockSpec((1, M, 2 * D), lambda b: (b, 0, 0)),
        out_shape=jax.ShapeDtypeStruct((B, M, 2 * D), BF16),
        compiler_params=_cparams(("parallel",)),
        name="mem_kv",
    )(mem, g, w_kv)


def _mix_kernel(x_ref, a_ref, c_ref, gs_ref, kv_ref, woa_ref, wco_ref, wout_ref, gx_ref,
                wq_ref, wox_ref, o_ref):
    dot = functools.partial(jnp.dot, preferred_element_type=F32)
    y_attn = dot(a_ref[0], woa_ref[...])
    y_conv = dot(c_ref[0], wco_ref[...])
    gs = gs_ref[0]
    merged = gs[:, :D_MODEL] * y_attn + gs[:, D_MODEL:] * y_conv
    x1 = x_ref[0] + dot(merged.astype(BF16), wout_ref[...])

    q = (dot(_rms(x1, gx_ref[...]).astype(BF16), wq_ref[...]) * (X_HEAD_DIM ** -0.5)).astype(BF16)
    heads = []
    for h in range(X_HEADS):
        lo = h * X_HEAD_DIM
        kh = kv_ref[0, :, lo:lo + X_HEAD_DIM]
        vh = kv_ref[0, :, D_MODEL + lo:D_MODEL + lo + X_HEAD_DIM]
        s = lax.dot_general(q[:, lo:lo + X_HEAD_DIM], kh, (((1,), (1,)), ((), ())),
                            preferred_element_type=F32)
        p = jnp.exp(s - jnp.max(s, axis=-1, keepdims=True))
        p = p / jnp.sum(p, axis=-1, keepdims=True)
        heads.append(dot(p.astype(BF16), vh))
    o = jnp.concatenate(heads, axis=-1)
    o_ref[0] = x1 + dot(o.astype(BF16), wox_ref[...])


def _mix_xattn(x, attn, uc, gs, kv, woa, wco, wout, gx, wq, wox):
    B, S, D = x.shape
    tm = TM_MIX
    row = lambda w: pl.BlockSpec((1, tm, w), lambda b, i: (b, i, 0))
    full = lambda a: pl.BlockSpec(a.shape, lambda b, i: (0,) * a.ndim)
    return pl.pallas_call(
        _mix_kernel,
        grid=(B, S // tm),
        in_specs=[row(D), row(ATTN_WIDTH), row(CONV_CH), row(2 * D),
                  pl.BlockSpec((1,) + kv.shape[1:], lambda b, i: (b, 0, 0)),
                  full(woa), full(wco), full(wout), full(gx), full(wq), full(wox)],
        out_specs=row(D),
        out_shape=jax.ShapeDtypeStruct((B, S, D), F32),
        compiler_params=_cparams(("parallel", "parallel")),
        name="mix_xattn",
    )(x, attn, uc, gs, kv, woa, wco, wout, gx, wq, wox)


def _moe_kernel(x_ref, gm_ref, wr_ref, br_ref, wg_ref, wu_ref, wd_ref, gf_ref, o_ref,
                hn_ref, comb_ref, y_ref):
    g = pl.program_id(1)
    tt = x_ref.shape[0]
    lane = lax.broadcasted_iota(I32, (tt, LANES), 1)

    @pl.when(g == 0)
    def _():
        hn = _rms(x_ref[...], gm_ref[...]).astype(BF16)
        hn_ref[...] = hn
        logits = jnp.dot(hn, wr_ref[...], preferred_element_type=F32) + br_ref[...]
        gl = jnp.where(lane < N_GROUPS, logits, -jnp.inf)
        gmax = jnp.max(gl, axis=-1, keepdims=True)
        g_sel = jnp.min(jnp.where(gl == gmax, lane, LANES), axis=-1, keepdims=True)
        g_gate = 1.0 / jnp.sum(jnp.exp(gl - gmax), axis=-1, keepdims=True)
        e_lo = N_GROUPS + g_sel * EXPERTS_PER_GROUP
        el = jnp.where((lane >= e_lo) & (lane < e_lo + EXPERTS_PER_GROUP), logits, -jnp.inf)
        top1 = jnp.max(el, axis=-1, keepdims=True)
        i1 = jnp.min(jnp.where(el == top1, lane, LANES), axis=-1, keepdims=True)
        el2 = jnp.where(lane == i1, -jnp.inf, el)
        top2 = jnp.max(el2, axis=-1, keepdims=True)
        i2 = jnp.min(jnp.where(el2 == top2, lane, LANES), axis=-1, keepdims=True)
        e2 = jnp.exp(top2 - top1)
        w1 = g_gate / (1.0 + e2)
        w2 = g_gate * e2 / (1.0 + e2)
        comb_ref[...] = jnp.where(lane == i1, w1, 0.0) + jnp.where(lane == i2, w2, 0.0)
        y_ref[...] = jnp.zeros_like(y_ref)

    hn = hn_ref[...]
    comb = comb_ref[...]
    y = y_ref[...]
    for e in range(EXPERTS_PER_GROUP):
        col = N_GROUPS + g * EXPERTS_PER_GROUP + e
        c = jnp.sum(jnp.where(lane == col, comb, 0.0), axis=-1, keepdims=True)
        gate = jnp.dot(hn, wg_ref[0, e], preferred_element_type=F32)
        up = jnp.dot(hn, wu_ref[0, e], preferred_element_type=F32)
        act = (gate * jax.nn.sigmoid(gate)) * up * c
        y = y + jnp.dot(act.astype(BF16), wd_ref[0, e], preferred_element_type=F32)
    y_ref[...] = y

    @pl.when(g == N_GROUPS - 1)
    def _():
        o_ref[...] = _rms(x_ref[...] + y_ref[...], gf_ref[...])


def _moe(x2d, gm, wr, br, wg, wu, wd, gf):
    T, D = x2d.shape
    tt = TT_MOE
    full = lambda a: pl.BlockSpec(a.shape, lambda i, g: (0,) * a.ndim)
    return pl.pallas_call(
        _moe_kernel,
        grid=(T // tt, N_GROUPS),
        in_specs=[pl.BlockSpec((tt, D), lambda i, g: (i, 0)), full(gm), full(wr), full(br),
                  pl.BlockSpec((1, EXPERTS_PER_GROUP, D, EXPERT_FF), lambda i, g: (g, 0, 0, 0)),
                  pl.BlockSpec((1, EXPERTS_PER_GROUP, D, EXPERT_FF), lambda i, g: (g, 0, 0, 0)),
                  pl.BlockSpec((1, EXPERTS_PER_GROUP, EXPERT_FF, D), lambda i, g: (g, 0, 0, 0)),
                  full(gf)],
        out_specs=pl.BlockSpec((tt, D), lambda i, g: (i, 0)),
        out_shape=jax.ShapeDtypeStruct((T, D), F32),
        scratch_shapes=[pltpu.VMEM((tt, D), BF16), pltpu.VMEM((tt, LANES), F32),
                        pltpu.VMEM((tt, D), F32)],
        compiler_params=_cparams(("parallel", "arbitrary")),
        name="moe",
    )(x2d, gm, wr, br, wg, wu, wd, gf)


def kernel(x, mem, positions, norm_mix_g, w_in, w_o_attn, conv_w, conv_b, conv_ln_g, conv_ln_b,
           w_conv_out, w_out, norm_x_g, norm_mem_g, w_q_x, w_kv_x, w_o_x, norm_moe_g,
           w_router_group, b_router_group, w_router_expert, b_router_expert,
           w_exp_gate, w_exp_up, w_exp_down, norm_final_g):
    B, S, D = x.shape
    T = B * S
    depth = norm_mix_g.shape[0]

    inv_freq = ROPE_THETA ** (-jnp.arange(0, HEAD_DIM, 2, dtype=F32) / HEAD_DIM)
    ang = positions.astype(F32)[..., None] * inv_freq
    cos, sin = jnp.cos(ang), jnp.sin(ang)
    cos128 = jnp.tile(cos, (1, 1, 4)).reshape(T, LANES)
    sin128 = jnp.tile(jnp.concatenate([-sin, sin], axis=-1), (1, 1, 2)).reshape(T, LANES)

    row = lambda a: a.reshape(1, -1)
    for l in range(depth):
        w = w_in[l]
        wm = jnp.concatenate([w[:, 0:512], w[:, 512:1024], w[:, 1536:2048],
                              w[:, 2120:3144], w[:, 3144:5192]], axis=1).astype(BF16)
        ws = jnp.pad(w[:, 2048:2112], ((0, 0), (0, LANES - IDX_DIM))).astype(BF16)
        wvt = w[:, 1024:1536].T.astype(BF16)
        wwt = w[:, 2112:2120].T.astype(BF16)
        q, k, qi, vt_blk, ki, wi_t, u, gs = _in_proj(x.reshape(T, D), row(norm_mix_g[l]), wm, ws, wvt, wwt,
                                                    cos128, sin128)
        attn = _dsa(q, qi, wi_t, ki, k, vt_blk, B, S).reshape(B, S, ATTN_WIDTH)

        uc = _conv(u.reshape(B, S, CONV_CH), jnp.pad(conv_w[l], ((0, 1), (0, 0))), row(conv_b[l]),
                   row(conv_ln_g[l]), row(conv_ln_b[l]))
        kv = _mem_kv(mem, row(norm_mem_g[l]), w_kv_x[l].astype(BF16))
        x = _mix_xattn(x, attn, uc, gs.reshape(B, S, 2 * D), kv,
                       w_o_attn[l].astype(BF16), w_conv_out[l].astype(BF16), w_out[l].astype(BF16),
                       row(norm_x_g[l]), w_q_x[l].astype(BF16), w_o_x[l].astype(BF16))

        wr = jnp.concatenate([w_router_group[l],
                              w_router_expert[l].transpose(1, 0, 2).reshape(D, N_EXPERTS)], axis=1)
        wr = jnp.pad(wr, ((0, 0), (0, LANES - N_GROUPS - N_EXPERTS))).astype(BF16)
        br = jnp.pad(jnp.concatenate([b_router_group[l], b_router_expert[l].reshape(-1)]),
                     (0, LANES - N_GROUPS - N_EXPERTS)).reshape(1, LANES)
        last = l == depth - 1
        gf = row(norm_final_g) if last else None
        assert last, "the final norm is fused into the last layer's MoE kernel"
        x = _moe(x.reshape(T, D), row(norm_moe_g[l]), wr, br, w_exp_gate[l].astype(BF16),
                 w_exp_up[l].astype(BF16), w_exp_down[l].astype(BF16), gf).reshape(B, S, D)
    return x
```

```python
import functools

import jax
import jax.numpy as jnp
from jax import lax
from jax.experimental import pallas as pl
from jax.experimental.pallas import tpu as pltpu

F32 = jnp.float32
BF16 = jnp.bfloat16
I32 = jnp.int32

D_MODEL = 1024
CHUNK = 64
ROPE_THETA = 10000.0
EPS = 1e-6
ATTN_HEADS = 8
HEAD_DIM = 64
ATTN_WIDTH = ATTN_HEADS * HEAD_DIM
IDX_HEADS = 8
IDX_DIM = 64
MAX_TOPK = 256
CONV_CH = D_MODEL // 2
CONV_WIDTH = 31
X_HEADS = 4
X_HEAD_DIM = D_MODEL // X_HEADS
N_GROUPS = 4
EXPERTS_PER_GROUP = 8
N_EXPERTS = N_GROUPS * EXPERTS_PER_GROUP
EXPERT_FF = D_MODEL // 4

LANES = 128
SUBLANES = 8
VMEM_LIMIT = 56 * 1024 * 1024
NEG_BIG = -1e30

TM_IN = 256
QT = 256
KB = 128
SEARCH_FIXED_STEPS = 14
ACC_ROWS = HEAD_DIM + 16
LOG2E = 1.4426950408889634
TC_CONV = 256
HALO = 32
TM_MIX = 256
TT_MOE = 1024


def _cparams(sem):
    return pltpu.CompilerParams(dimension_semantics=sem, vmem_limit_bytes=VMEM_LIMIT)


def _rms(x, g):
    return x * lax.rsqrt(jnp.mean(x * x, axis=-1, keepdims=True) + EPS) * g


def _inproj_kernel(x_ref, g_ref, wm_ref, ws_ref, wvt_ref, wwt_ref, cos_ref, sin_ref,
                   q_ref, k_ref, qi_ref, vt_ref, ki_ref, wit_ref, u_ref, gs_ref):
    tm = x_ref.shape[0]
    nt_dims = (((1,), (1,)), ((), ()))
    h = _rms(x_ref[...], g_ref[...]).astype(BF16)
    cos = cos_ref[...]
    sin = sin_ref[...]
    lane = lax.broadcasted_iota(I32, (tm, LANES), 1)
    first_half = (lane % HEAD_DIM) < (HEAD_DIM // 2)

    def rope(y):
        rot = jnp.where(first_half, pltpu.roll(y, LANES - HEAD_DIM // 2, 1),
                        pltpu.roll(y, HEAD_DIM // 2, 1))
        return y * cos + rot * sin

    def proj(c0, w):
        return jnp.dot(h, wm_ref[:, c0:c0 + w], preferred_element_type=F32)

    for ref, c0, scale in ((q_ref, 0, HEAD_DIM ** -0.5 * LOG2E), (k_ref, 512, None), (qi_ref, 1024, None)):
        y = proj(c0, ATTN_WIDTH)
        for c in range(ATTN_WIDTH // LANES):
            r = rope(y[:, c * LANES:(c + 1) * LANES])
            if scale is not None:
                r = r * scale
            ref[:, c * LANES:(c + 1) * LANES] = r.astype(BF16)
    v_t = lax.dot_general(wvt_ref[...], h, nt_dims, preferred_element_type=F32)
    for c in range(tm // KB):
        vt_ref[c] = v_t[:, c * KB:(c + 1) * KB].astype(BF16)
    wit_ref[...] = lax.dot_general(wwt_ref[...], h, nt_dims, preferred_element_type=F32)
    glu = proj(1536, 2 * CONV_CH)
    u_ref[...] = glu[:, :CONV_CH] * jax.nn.sigmoid(glu[:, CONV_CH:])
    for c in range(4):
        gs_ref[:, c * 512:(c + 1) * 512] = jax.nn.sigmoid(proj(2560 + c * 512, 512))
    ys = jnp.dot(h, ws_ref[...], preferred_element_type=F32)
    ki_ref[...] = rope(ys)[:, :IDX_DIM].astype(BF16)


def _in_proj(x2d, g, wm, ws, wvt, wwt, cos128, sin128):
    T = x2d.shape[0]
    tm = TM_IN
    row = lambda w: pl.BlockSpec((tm, w), lambda i: (i, 0))
    full = lambda a: pl.BlockSpec(a.shape, lambda i: (0,) * a.ndim)
    return pl.pallas_call(
        _inproj_kernel,
        grid=(T // tm,),
        in_specs=[row(D_MODEL), full(g), full(wm), full(ws), full(wvt), full(wwt), row(LANES), row(LANES)],
        out_specs=[row(512), row(512), row(512),
                   pl.BlockSpec((tm // KB, ATTN_WIDTH, KB), lambda i: (i, 0, 0)),
                   row(IDX_DIM), pl.BlockSpec((IDX_HEADS, tm), lambda i: (0, i)),
                   row(CONV_CH), row(2 * D_MODEL)],
        out_shape=[jax.ShapeDtypeStruct((T, 512), BF16)] * 3
        + [jax.ShapeDtypeStruct((T // KB, ATTN_WIDTH, KB), BF16),
           jax.ShapeDtypeStruct((T, IDX_DIM), BF16), jax.ShapeDtypeStruct((IDX_HEADS, T), F32),
           jax.ShapeDtypeStruct((T, CONV_CH), F32), jax.ShapeDtypeStruct((T, 2 * D_MODEL), F32)],
        compiler_params=_cparams(("parallel",)),
        name="in_proj",
    )(x2d, g, wm, ws, wvt, wwt, cos128, sin128)


def _dsa_kernel(q_ref, qi_ref, wi_ref, ki_ref, k_ref, vt_ref, o_ref,
                sc_ref, bias_ref, acc_ref, sa_ref, sb_ref, pa_ref, pb_ref, qh_ref, qih_ref):
    assert QT == 2 * KB
    t = pl.program_id(1)
    n_qb = t + 1
    n_kb = 2 * n_qb
    scale_idx = (IDX_DIM ** -0.5) * (IDX_HEADS ** -0.5)
    nt_dims = (((1,), (1,)), ((), ()))
    top_k = float(MAX_TOPK)
    head = lambda h: slice(h * HEAD_DIM, (h + 1) * HEAD_DIM)

    qcol = lax.broadcasted_iota(I32, (1, QT), 1) + t * QT
    limit = (qcol // CHUNK + 1) * CHUNK
    w_idx = wi_ref[...]
    for h in range(ATTN_HEADS):
        qh_ref[h] = q_ref[:, head(h)]
        qih_ref[h] = qi_ref[:, head(h)]

    def score_rows(ks, masked):
        ki_blk = ki_ref[pl.ds(ks, KB), :]
        acc = jnp.zeros((KB, QT), F32)
        for h in range(IDX_HEADS):
            rel = lax.dot_general(ki_blk, qih_ref[h], nt_dims, preferred_element_type=F32)
            acc = acc + jnp.maximum(rel, 0.0) * w_idx[h:h + 1, :]
        score = acc * scale_idx
        if masked:
            adm = lax.broadcasted_iota(I32, (KB, QT), 0) + ks < limit
            lo_src = jnp.where(adm, score, jnp.inf)
            score = jnp.where(adm, score, -jnp.inf)
        else:
            lo_src = score
        sc_ref[pl.ds(ks, KB), :] = score
        fold = lambda a, op: functools.reduce(op, [a[r * 8:(r + 1) * 8] for r in range(KB // 8)])
        return fold(lo_src, jnp.minimum), fold(score, jnp.maximum)

    def score_block(qb, carry, masked=False):
        mn, mx = carry
        for sub in range(QT // KB):
            ks = pl.multiple_of(qb * QT + sub * KB, KB)
            bmn, bmx = score_rows(ks, masked)
            mn, mx = jnp.minimum(mn, bmn), jnp.maximum(mx, bmx)
        return mn, mx

    init = (jnp.full((8, QT), jnp.inf, F32), jnp.full((8, QT), -jnp.inf, F32))
    mn8, mx8 = score_block(t, lax.fori_loop(0, t, score_block, init), masked=True)
    s_min = jnp.min(mn8, axis=0, keepdims=True)
    s_max = jnp.max(mx8, axis=0, keepdims=True)

    n_pairs = (n_qb + 1) // 2

    @pl.when(n_qb % 2 == 1)
    def _():
        sc_ref[pl.ds(pl.multiple_of(n_qb * QT, QT), QT), :] = jnp.full((QT, QT), -jnp.inf, F32)

    def count(cand, strict=False):
        def body(pb, c):
            for sub in range(2):
                qs = pl.multiple_of((2 * pb + sub) * QT, QT)
                blk = sc_ref[pl.ds(qs, QT), :]
                hit = jnp.where(blk > cand if strict else blk >= cand, 1.0, 0.0)
                c = c + jnp.sum(hit.reshape(QT // 8, 8, QT), axis=0)
            return c
        c = lax.fori_loop(0, n_pairs, body, jnp.zeros((8, QT), F32))
        return jnp.sum(c, axis=0, keepdims=True)

    count_ge = count

    def search_step(c):
        lo, hi, c_lo, active = c
        mid = 0.5 * lo + 0.5 * hi
        cnt = count_ge(mid)
        up = (active > 0.0) & (cnt >= top_k)
        down = (active > 0.0) & (cnt < top_k)
        stuck = (mid <= lo) | (mid >= hi)
        lo, c_lo, hi = jnp.where(up, mid, lo), jnp.where(up, cnt, c_lo), jnp.where(down, mid, hi)
        active = jnp.where((c_lo > top_k) & jnp.logical_not(stuck), active, 0.0)
        return lo, hi, c_lo, active

    n_adm = limit.astype(F32)
    searching = n_adm > top_k
    zero = jnp.zeros((1, QT), F32)
    c_ge0, c_gt0 = count(zero), count(zero, strict=True)
    above = searching & (c_gt0 >= top_k)
    at_zero = searching & (c_gt0 < top_k) & (c_ge0 >= top_k)
    below = searching & (c_ge0 < top_k)
    state = (jnp.where(above | at_zero, zero, s_min), jnp.where(below | at_zero, zero, s_max),
             jnp.where(above | at_zero, c_ge0, n_adm),
             jnp.where(above | below, 1.0, 0.0))
    state = lax.fori_loop(0, SEARCH_FIXED_STEPS, lambda i, c: search_step(c), state)
    lo, hi, c_lo, _, _ = lax.while_loop(
        lambda c: jnp.logical_and(jnp.max(c[3]) > 0.0, c[4] < 200),
        lambda c: search_step(search_step(c[:4])) + (c[4] + 1,),
        state + (jnp.int32(0),))

    def bias_block(qb, carry):
        qs = pl.multiple_of(qb * QT, QT)
        bias_ref[pl.ds(qs, QT), :] = jnp.where(sc_ref[pl.ds(qs, QT), :] >= lo, 0.0, NEG_BIG)
        return carry

    lax.fori_loop(0, n_qb, bias_block, 0)

    @pl.when(jnp.max(c_lo) > top_k)
    def _():
        tau = jnp.where(count_ge(hi) >= top_k, hi, lo)
        need = top_k - count(tau, strict=True)
        r = lax.broadcasted_iota(I32, (QT, QT), 0)
        c = lax.broadcasted_iota(I32, (QT, QT), 1)
        lower = jnp.where(r > c, 1.0, 0.0).astype(BF16)

        def tie_block(qb, seen):
            qs = pl.multiple_of(qb * QT, QT)
            blk = sc_ref[pl.ds(qs, QT), :]
            eq = blk == tau
            eqf = jnp.where(eq, 1.0, 0.0)
            rank = jnp.dot(lower, eqf.astype(BF16), preferred_element_type=F32) + seen
            sel = (blk > tau) | (eq & (rank < need))
            bias_ref[pl.ds(qs, QT), :] = jnp.where(sel, 0.0, NEG_BIG)
            return seen + jnp.sum(eqf, axis=0, keepdims=True)

        lax.fori_loop(0, n_qb, tie_block, jnp.zeros((1, QT), F32))

    def logits(kb, s_ref):
        ks = pl.multiple_of(kb * KB, KB)
        bias = bias_ref[pl.ds(ks, KB), :]
        for h in range(ATTN_HEADS):
            s_ref[h] = lax.dot_general(k_ref[pl.ds(ks, KB), head(h)], qh_ref[h], nt_dims,
                                       preferred_element_type=F32) + bias

    def softmax(s_ref, p_ref, m):
        ms, alphas = [], []
        for h in range(ATTN_HEADS):
            s = s_ref[h]
            m_old = m[h:h + 1]
            m_new = jnp.maximum(m_old, jnp.max(s, axis=0, keepdims=True))
            p_ref[h] = jnp.exp2(s - m_new).astype(BF16)
            ms.append(m_new)
            alphas.append(jnp.exp2(m_old - m_new))
        cat = lambda rows: jnp.concatenate(rows, axis=0)
        return cat(ms), cat(alphas)

    ones_rows = jnp.ones((ACC_ROWS - HEAD_DIM, KB), BF16)

    def values(kb, p_ref, alpha):
        for h in range(ATTN_HEADS):
            lhs = jnp.concatenate([vt_ref[kb, head(h), :], ones_rows], axis=0)
            acc_ref[h] = acc_ref[h] * alpha[h:h + 1] + jnp.dot(lhs, p_ref[h],
                                                              preferred_element_type=F32)

    acc_ref[...] = jnp.zeros(acc_ref.shape, F32)
    pb_ref[...] = jnp.zeros(pb_ref.shape, BF16)
    logits(0, sa_ref)

    def attn_pair(j, carry):
        m, alpha_b = carry
        kb0 = 2 * j
        logits(kb0 + 1, sb_ref)
        m, alpha_a = softmax(sa_ref, pa_ref, m)
        values(jnp.maximum(kb0 - 1, 0), pb_ref, alpha_b)
        logits(jnp.minimum(kb0 + 2, n_kb - 1), sa_ref)
        m, alpha_b = softmax(sb_ref, pb_ref, m)
        values(kb0, pa_ref, alpha_a)
        return m, alpha_b

    heads_qt = (ATTN_HEADS, QT)
    _, alpha_b = lax.fori_loop(0, n_qb, attn_pair,
                               (jnp.full(heads_qt, NEG_BIG, F32), jnp.ones(heads_qt, F32)))
    values(n_kb - 1, pb_ref, alpha_b)
    for j in range(ATTN_HEADS // 2):
        pair = jnp.concatenate(
            [acc_ref[h, :HEAD_DIM, :] / acc_ref[h, HEAD_DIM:HEAD_DIM + 1, :] for h in (2 * j, 2 * j + 1)],
            axis=0)
        o_ref[:, j * LANES:(j + 1) * LANES] = pair.T.astype(BF16)


def _dsa(q, qi, wi_t, ki, k, vt_blk, B, S):
    nq = S // QT
    return pl.pallas_call(
        _dsa_kernel,
        grid=(B, nq),
        in_specs=[
            pl.BlockSpec((QT, ATTN_WIDTH), lambda b, t: (b * nq + t, 0)),
            pl.BlockSpec((QT, IDX_HEADS * IDX_DIM), lambda b, t: (b * nq + t, 0)),
            pl.BlockSpec((IDX_HEADS, QT), lambda b, t: (0, b * nq + t)),
            pl.BlockSpec((S, IDX_DIM), lambda b, t: (b, 0)),
            pl.BlockSpec((S, ATTN_WIDTH), lambda b, t: (b, 0)),
            pl.BlockSpec((S // KB, ATTN_WIDTH, KB), lambda b, t: (b, 0, 0)),
        ],
        out_specs=pl.BlockSpec((QT, ATTN_WIDTH), lambda b, t: (b * nq + t, 0)),
        out_shape=jax.ShapeDtypeStruct((B * S, ATTN_WIDTH), BF16),
        scratch_shapes=[pltpu.VMEM((S, QT), F32), pltpu.VMEM((S, QT), F32),
                        pltpu.VMEM((ATTN_HEADS, ACC_ROWS, QT), F32),
                        pltpu.VMEM((ATTN_HEADS, KB, QT), F32), pltpu.VMEM((ATTN_HEADS, KB, QT), F32),
                        pltpu.VMEM((ATTN_HEADS, KB, QT), BF16), pltpu.VMEM((ATTN_HEADS, KB, QT), BF16),
                        pltpu.VMEM((ATTN_HEADS, QT, HEAD_DIM), BF16),
                        pltpu.VMEM((IDX_HEADS, QT, IDX_DIM), BF16)],
        compiler_params=_cparams(("parallel", "arbitrary")),
        name="dsa",
    )(q, qi, wi_t, ki, k, vt_blk)


def _conv_kernel(u_ref, halo_ref, w_ref, b_ref, lg_ref, lb_ref, o_ref, buf_ref):
    i = pl.program_id(1)
    tc = u_ref.shape[1]
    halo = halo_ref[0]
    buf_ref[0, 0:HALO, :] = jnp.where(i > 0, halo, jnp.zeros_like(halo))
    buf_ref[0, HALO:HALO + tc, :] = u_ref[0]
    span = HALO + tc - SUBLANES
    for r in range(1, SUBLANES):
        buf_ref[r, 0:span, :] = buf_ref[0, r:r + span, :]
    off = HALO - (CONV_WIDTH - 1)
    rows = 64
    for rb in range(tc // rows):
        acc = jnp.zeros((rows, CONV_CH), F32) + b_ref[...]
        for j in range(CONV_WIDTH):
            r = (off + j) % SUBLANES
            s = rb * rows + off + j - r
            acc = acc + buf_ref[r, s:s + rows, :] * w_ref[j:j + 1, :]
        mu = jnp.mean(acc, axis=-1, keepdims=True)
        d = acc - mu
        var = jnp.mean(d * d, axis=-1, keepdims=True)
        y = d * lax.rsqrt(var + EPS) * lg_ref[...] + lb_ref[...]
        o_ref[0, rb * rows:(rb + 1) * rows, :] = (y * jax.nn.sigmoid(y)).astype(BF16)


def _conv(u, conv_w, conv_b, ln_g, ln_b):
    B, S, C = u.shape
    tc = TC_CONV
    full = lambda a: pl.BlockSpec(a.shape, lambda b, i: (0,) * a.ndim)
    return pl.pallas_call(
        _conv_kernel,
        grid=(B, S // tc),
        in_specs=[
            pl.BlockSpec((1, tc, C), lambda b, i: (b, i, 0)),
            pl.BlockSpec((1, HALO, C), lambda b, i: (b, jnp.maximum(i * (tc // HALO) - 1, 0), 0)),
            full(conv_w), full(conv_b), full(ln_g), full(ln_b),
        ],
        out_specs=pl.BlockSpec((1, tc, C), lambda b, i: (b, i, 0)),
        out_shape=jax.ShapeDtypeStruct((B, S, C), BF16),
        scratch_shapes=[pltpu.VMEM((SUBLANES, HALO + tc, C), F32)],
        compiler_params=_cparams(("parallel", "parallel")),
        name="conv",
    )(u, u, conv_w, conv_b, ln_g, ln_b)


def _memkv_kernel(mem_ref, g_ref, w_ref, o_ref):
    h = _rms(mem_ref[0], g_ref[...]).astype(BF16)
    o_ref[0] = jnp.dot(h, w_ref[...], preferred_element_type=F32).astype(BF16)


def _mem_kv(mem, g, w_kv):
    B, M, D = mem.shape
    return pl.pallas_call(
        _memkv_kernel,
        grid=(B,),
        in_specs=[pl.BlockSpec((1, M, D), lambda b: (b, 0, 0)),
                  pl.BlockSpec(g.shape, lambda b: (0, 0)),
                  pl.BlockSpec(w_kv.shape, lambda b: (0, 0))],
        out_specs=pl.BlockSpec((1, M, 2 * D), lambda b: (b, 0, 0)),
        out_shape=jax.ShapeDtypeStruct((B, M, 2 * D), BF16),
        compiler_params=_cparams(("parallel",)),
        name="mem_kv",
    )(mem, g, w_kv)


def _mix_kernel(x_ref, a_ref, c_ref, gs_ref, kv_ref, woa_ref, wco_ref, wout_ref, gx_ref,
                wq_ref, wox_ref, o_ref):
    dot = functools.partial(jnp.dot, preferred_element_type=F32)
    y_attn = dot(a_ref[0], woa_ref[...])
    y_conv = dot(c_ref[0], wco_ref[...])
    gs = gs_ref[0]
    merged = gs[:, :D_MODEL] * y_attn + gs[:, D_MODEL:] * y_conv
    x1 = x_ref[0] + dot(merged.astype(BF16), wout_ref[...])

    q = (dot(_rms(x1, gx_ref[...]).astype(BF16), wq_ref[...]) * (X_HEAD_DIM ** -0.5)).astype(BF16)
    heads = []
    for h in range(X_HEADS):
        lo = h * X_HEAD_DIM
        kh = kv_ref[0, :, lo:lo + X_HEAD_DIM]
        vh = kv_ref[0, :, D_MODEL + lo:D_MODEL + lo + X_HEAD_DIM]
        s = lax.dot_general(q[:, lo:lo + X_HEAD_DIM], kh, (((1,), (1,)), ((), ())),
                            preferred_element_type=F32)
        p = jnp.exp(s - jnp.max(s, axis=-1, keepdims=True))
        p = p / jnp.sum(p, axis=-1, keepdims=True)
        heads.append(dot(p.astype(BF16), vh))
    o = jnp.concatenate(heads, axis=-1)
    o_ref[0] = x1 + dot(o.astype(BF16), wox_ref[...])


def _mix_xattn(x, attn, uc, gs, kv, woa, wco, wout, gx, wq, wox):
    B, S, D = x.shape
    tm = TM_MIX
    row = lambda w: pl.BlockSpec((1, tm, w), lambda b, i: (b, i, 0))
    full = lambda a: pl.BlockSpec(a.shape, lambda b, i: (0,) * a.ndim)
    return pl.pallas_call(
        _mix_kernel,
        grid=(B, S // tm),
        in_specs=[row(D), row(ATTN_WIDTH), row(CONV_CH), row(2 * D),
                  pl.BlockSpec((1,) + kv.shape[1:], lambda b, i: (b, 0, 0)),
                  full(woa), full(wco), full(wout), full(gx), full(wq), full(wox)],
        out_specs=row(D),
        out_shape=jax.ShapeDtypeStruct((B, S, D), F32),
        compiler_params=_cparams(("parallel", "parallel")),
        name="mix_xattn",
    )(x, attn, uc, gs, kv, woa, wco, wout, gx, wq, wox)


def _moe_kernel(x_ref, gm_ref, wr_ref, br_ref, wg_ref, wu_ref, wd_ref, gf_ref, o_ref,
                hn_ref, comb_ref, y_ref):
    g = pl.program_id(1)
    tt = x_ref.shape[0]
    lane = lax.broadcasted_iota(I32, (tt, LANES), 1)

    @pl.when(g == 0)
    def _():
        hn = _rms(x_ref[...], gm_ref[...]).astype(BF16)
        hn_ref[...] = hn
        logits = jnp.dot(hn, wr_ref[...], preferred_element_type=F32) + br_ref[...]
        gl = jnp.where(lane < N_GROUPS, logits, -jnp.inf)
        gmax = jnp.max(gl, axis=-1, keepdims=True)
        g_sel = jnp.min(jnp.where(gl == gmax, lane, LANES), axis=-1, keepdims=True)
        g_gate = 1.0 / jnp.sum(jnp.exp(gl - gmax), axis=-1, keepdims=True)
        e_lo = N_GROUPS + g_sel * EXPERTS_PER_GROUP
        el = jnp.where((lane >= e_lo) & (lane < e_lo + EXPERTS_PER_GROUP), logits, -jnp.inf)
        top1 = jnp.max(el, axis=-1, keepdims=True)
        i1 = jnp.min(jnp.where(el == top1, lane, LANES), axis=-1, keepdims=True)
        el2 = jnp.where(lane == i1, -jnp.inf, el)
        top2 = jnp.max(el2, axis=-1, keepdims=True)
        i2 = jnp.min(jnp.where(el2 == top2, lane, LANES), axis=-1, keepdims=True)
        e2 = jnp.exp(top2 - top1)
        w1 = g_gate / (1.0 + e2)
        w2 = g_gate * e2 / (1.0 + e2)
        comb_ref[...] = jnp.where(lane == i1, w1, 0.0) + jnp.where(lane == i2, w2, 0.0)
        y_ref[...] = jnp.zeros_like(y_ref)

    hn = hn_ref[...]
    comb = comb_ref[...]
    y = y_ref[...]
    for e in range(EXPERTS_PER_GROUP):
        col = N_GROUPS + g * EXPERTS_PER_GROUP + e
        c = jnp.sum(jnp.where(lane == col, comb, 0.0), axis=-1, keepdims=True)
        gate = jnp.dot(hn, wg_ref[0, e], preferred_element_type=F32)
        up = jnp.dot(hn, wu_ref[0, e], preferred_element_type=F32)
        act = (gate * jax.nn.sigmoid(gate)) * up * c
        y = y + jnp.dot(act.astype(BF16), wd_ref[0, e], preferred_element_type=F32)
    y_ref[...] = y

    @pl.when(g == N_GROUPS - 1)
    def _():
        o_ref[...] = _rms(x_ref[...] + y_ref[...], gf_ref[...])


def _moe(x2d, gm, wr, br, wg, wu, wd, gf):
    T, D = x2d.shape
    tt = TT_MOE
    full = lambda a: pl.BlockSpec(a.shape, lambda i, g: (0,) * a.ndim)
    return pl.pallas_call(
        _moe_kernel,
        grid=(T // tt, N_GROUPS),
        in_specs=[pl.BlockSpec((tt, D), lambda i, g: (i, 0)), full(gm), full(wr), full(br),
                  pl.BlockSpec((1, EXPERTS_PER_GROUP, D, EXPERT_FF), lambda i, g: (g, 0, 0, 0)),
                  pl.BlockSpec((1, EXPERTS_PER_GROUP, D, EXPERT_FF), lambda i, g: (g, 0, 0, 0)),
                  pl.BlockSpec((1, EXPERTS_PER_GROUP, EXPERT_FF, D), lambda i, g: (g, 0, 0, 0)),
                  full(gf)],
        out_specs=pl.BlockSpec((tt, D), lambda i, g: (i, 0)),
        out_shape=jax.ShapeDtypeStruct((T, D), F32),
        scratch_shapes=[pltpu.VMEM((tt, D), BF16), pltpu.VMEM((tt, LANES), F32),
                        pltpu.VMEM((tt, D), F32)],
        compiler_params=_cparams(("parallel", "arbitrary")),
        name="moe",
    )(x2d, gm, wr, br, wg, wu, wd, gf)


def kernel(x, mem, positions, norm_mix_g, w_in, w_o_attn, conv_w, conv_b, conv_ln_g, conv_ln_b,
           w_conv_out, w_out, norm_x_g, norm_mem_g, w_q_x, w_kv_x, w_o_x, norm_moe_g,
           w_router_group, b_router_group, w_router_expert, b_router_expert,
           w_exp_gate, w_exp_up, w_exp_down, norm_final_g):
    B, S, D = x.shape
    T = B * S
    depth = norm_mix_g.shape[0]

    inv_freq = ROPE_THETA ** (-jnp.arange(0, HEAD_DIM, 2, dtype=F32) / HEAD_DIM)
    ang = positions.astype(F32)[..., None] * inv_freq
    cos, sin = jnp.cos(ang), jnp.sin(ang)
    cos128 = jnp.tile(cos, (1, 1, 4)).reshape(T, LANES)
    sin128 = jnp.tile(jnp.concatenate([-sin, sin], axis=-1), (1, 1, 2)).reshape(T, LANES)

    row = lambda a: a.reshape(1, -1)
    for l in range(depth):
        w = w_in[l]
        wm = jnp.concatenate([w[:, 0:512], w[:, 512:1024], w[:, 1536:2048],
                              w[:, 2120:3144], w[:, 3144:5192]], axis=1).astype(BF16)
        ws = jnp.pad(w[:, 2048:2112], ((0, 0), (0, LANES - IDX_DIM))).astype(BF16)
        wvt = w[:, 1024:1536].T.astype(BF16)
        wwt = w[:, 2112:2120].T.astype(BF16)
        q, k, qi, vt_blk, ki, wi_t, u, gs = _in_proj(x.reshape(T, D), row(norm_mix_g[l]), wm, ws, wvt, wwt,
                                                    cos128, sin128)
        attn = _dsa(q, qi, wi_t, ki, k, vt_blk, B, S).reshape(B, S, ATTN_WIDTH)

        uc = _conv(u.reshape(B, S, CONV_CH), jnp.pad(conv_w[l], ((0, 1), (0, 0))), row(conv_b[l]),
                   row(conv_ln_g[l]), row(conv_ln_b[l]))
        kv = _mem_kv(mem, row(norm_mem_g[l]), w_kv_x[l].astype(BF16))
        x = _mix_xattn(x, attn, uc, gs.reshape(B, S, 2 * D), kv,
                       w_o_attn[l].astype(BF16), w_conv_out[l].astype(BF16), w_out[l].astype(BF16),
                       row(norm_x_g[l]), w_q_x[l].astype(BF16), w_o_x[l].astype(BF16))

        wr = jnp.concatenate([w_router_group[l],
                              w_router_expert[l].transpose(1, 0, 2).reshape(D, N_EXPERTS)], axis=1)
        wr = jnp.pad(wr, ((0, 0), (0, LANES - N_GROUPS - N_EXPERTS))).astype(BF16)
        br = jnp.pad(jnp.concatenate([b_router_group[l], b_router_expert[l].reshape(-1)]),
                     (0, LANES - N_GROUPS - N_EXPERTS)).reshape(1, LANES)
        last = l == depth - 1
        gf = row(norm_final_g) if last else None
        assert last, "the final norm is fused into the last layer's MoE kernel"
        x = _moe(x.reshape(T, D), row(norm_moe_g[l]), wr, br, w_exp_gate[l].astype(BF16),
                 w_exp_up[l].astype(BF16), w_exp_down[l].astype(BF16), gf).reshape(B, S, D)
    return x
```

```python
import functools

import jax
import jax.numpy as jnp
from jax import lax
from jax.experimental import pallas as pl
from jax.experimental.pallas import tpu as pltpu

F32 = jnp.float32
BF16 = jnp.bfloat16
I32 = jnp.int32

D_MODEL = 1024
CHUNK = 64
ROPE_THETA = 10000.0
EPS = 1e-6
ATTN_HEADS = 8
HEAD_DIM = 64
ATTN_WIDTH = ATTN_HEADS * HEAD_DIM
IDX_HEADS = 8
IDX_DIM = 64
MAX_TOPK = 256
CONV_CH = D_MODEL // 2
CONV_WIDTH = 31
X_HEADS = 4
X_HEAD_DIM = D_MODEL // X_HEADS
N_GROUPS = 4
EXPERTS_PER_GROUP = 8
N_EXPERTS = N_GROUPS * EXPERTS_PER_GROUP
EXPERT_FF = D_MODEL // 4

LANES = 128
SUBLANES = 8
VMEM_LIMIT = 56 * 1024 * 1024
NEG_BIG = -1e30

TM_IN = 256
QT = 256
KB = 128
SEARCH_FIXED_STEPS = 14
ACC_ROWS = HEAD_DIM + 16
LOG2E = 1.4426950408889634
TC_CONV = 256
HALO = 32
TM_MIX = 512
TT_MOE = 1024


def _cparams(sem):
    return pltpu.CompilerParams(dimension_semantics=sem, vmem_limit_bytes=VMEM_LIMIT)


def _rms(x, g):
    return x * lax.rsqrt(jnp.mean(x * x, axis=-1, keepdims=True) + EPS) * g


def _inproj_kernel(x_ref, g_ref, wm_ref, ws_ref, wvt_ref, wwt_ref, cos_ref, sin_ref,
                   q_ref, k_ref, qi_ref, vt_ref, ki_ref, wit_ref, u_ref, gs_ref):
    tm = x_ref.shape[0]
    nt_dims = (((1,), (1,)), ((), ()))
    h = _rms(x_ref[...], g_ref[...]).astype(BF16)
    cos = cos_ref[...]
    sin = sin_ref[...]
    lane = lax.broadcasted_iota(I32, (tm, LANES), 1)
    first_half = (lane % HEAD_DIM) < (HEAD_DIM // 2)

    def rope(y):
        rot = jnp.where(first_half, pltpu.roll(y, LANES - HEAD_DIM // 2, 1),
                        pltpu.roll(y, HEAD_DIM // 2, 1))
        return y * cos + rot * sin

    def proj(c0, w):
        return jnp.dot(h, wm_ref[:, c0:c0 + w], preferred_element_type=F32)

    for ref, c0, scale in ((q_ref, 0, HEAD_DIM ** -0.5 * LOG2E), (k_ref, 512, None), (qi_ref, 1024, None)):
        y = proj(c0, ATTN_WIDTH)
        for c in range(ATTN_WIDTH // LANES):
            r = rope(y[:, c * LANES:(c + 1) * LANES])
            if scale is not None:
                r = r * scale
            ref[:, c * LANES:(c + 1) * LANES] = r.astype(BF16)
    v_t = lax.dot_general(wvt_ref[...], h, nt_dims, preferred_element_type=F32)
    for c in range(tm // KB):
        vt_ref[c] = v_t[:, c * KB:(c + 1) * KB].astype(BF16)
    wit_ref[...] = lax.dot_general(wwt_ref[...], h, nt_dims, preferred_element_type=F32)
    glu = proj(1536, 2 * CONV_CH)
    u_ref[...] = glu[:, :CONV_CH] * jax.nn.sigmoid(glu[:, CONV_CH:])
    for c in range(4):
        gs_ref[:, c * 512:(c + 1) * 512] = jax.nn.sigmoid(proj(2560 + c * 512, 512))
    ys = jnp.dot(h, ws_ref[...], preferred_element_type=F32)
    ki_ref[...] = rope(ys)[:, :IDX_DIM].astype(BF16)


def _in_proj(x2d, g, wm, ws, wvt, wwt, cos128, sin128):
    T = x2d.shape[0]
    tm = TM_IN
    row = lambda w: pl.BlockSpec((tm, w), lambda i: (i, 0))
    full = lambda a: pl.BlockSpec(a.shape, lambda i: (0,) * a.ndim)
    return pl.pallas_call(
        _inproj_kernel,
        grid=(T // tm,),
        in_specs=[row(D_MODEL), full(g), full(wm), full(ws), full(wvt), full(wwt), row(LANES), row(LANES)],
        out_specs=[row(512), row(512), row(512),
                   pl.BlockSpec((tm // KB, ATTN_WIDTH, KB), lambda i: (i, 0, 0)),
                   row(IDX_DIM), pl.BlockSpec((IDX_HEADS, tm), lambda i: (0, i)),
                   row(CONV_CH), row(2 * D_MODEL)],
        out_shape=[jax.ShapeDtypeStruct((T, 512), BF16)] * 3
        + [jax.ShapeDtypeStruct((T // KB, ATTN_WIDTH, KB), BF16),
           jax.ShapeDtypeStruct((T, IDX_DIM), BF16), jax.ShapeDtypeStruct((IDX_HEADS, T), F32),
           jax.ShapeDtypeStruct((T, CONV_CH), F32), jax.ShapeDtypeStruct((T, 2 * D_MODEL), F32)],
        compiler_params=_cparams(("parallel",)),
        name="in_proj",
    )(x2d, g, wm, ws, wvt, wwt, cos128, sin128)


def _dsa_kernel(q_ref, qi_ref, wi_ref, ki_ref, k_ref, vt_ref, o_ref,
                sc_ref, bias_ref, acc_ref, sa_ref, sb_ref, pa_ref, pb_ref, qh_ref, qih_ref):
    assert QT == 2 * KB
    t = pl.program_id(1)
    n_qb = t + 1
    n_kb = 2 * n_qb
    scale_idx = (IDX_DIM ** -0.5) * (IDX_HEADS ** -0.5)
    nt_dims = (((1,), (1,)), ((), ()))
    top_k = float(MAX_TOPK)
    head = lambda h: slice(h * HEAD_DIM, (h + 1) * HEAD_DIM)

    qcol = lax.broadcasted_iota(I32, (1, QT), 1) + t * QT
    limit = (qcol // CHUNK + 1) * CHUNK
    w_idx = wi_ref[...]
    for h in range(ATTN_HEADS):
        qh_ref[h] = q_ref[:, head(h)]
        qih_ref[h] = qi_ref[:, head(h)]

    def score_rows(ks, masked):
        ki_blk = ki_ref[pl.ds(ks, KB), :]
        acc = jnp.zeros((KB, QT), F32)
        for h in range(IDX_HEADS):
            rel = lax.dot_general(ki_blk, qih_ref[h], nt_dims, preferred_element_type=F32)
            acc = acc + jnp.maximum(rel, 0.0) * w_idx[h:h + 1, :]
        score = acc * scale_idx
        if masked:
            adm = lax.broadcasted_iota(I32, (KB, QT), 0) + ks < limit
            lo_src = jnp.where(adm, score, jnp.inf)
            score = jnp.where(adm, score, -jnp.inf)
        else:
            lo_src = score
        sc_ref[pl.ds(ks, KB), :] = score
        fold = lambda a, op: functools.reduce(op, [a[r * 8:(r + 1) * 8] for r in range(KB // 8)])
        return fold(lo_src, jnp.minimum), fold(score, jnp.maximum)

    def score_block(qb, carry, masked=False):
        mn, mx = carry
        for sub in range(QT // KB):
            ks = pl.multiple_of(qb * QT + sub * KB, KB)
            bmn, bmx = score_rows(ks, masked)
            mn, mx = jnp.minimum(mn, bmn), jnp.maximum(mx, bmx)
        return mn, mx

    init = (jnp.full((8, QT), jnp.inf, F32), jnp.full((8, QT), -jnp.inf, F32))
    mn8, mx8 = score_block(t, lax.fori_loop(0, t, score_block, init), masked=True)
    s_min = jnp.min(mn8, axis=0, keepdims=True)
    s_max = jnp.max(mx8, axis=0, keepdims=True)

    n_pairs = (n_qb + 1) // 2

    @pl.when(n_qb % 2 == 1)
    def _():
        sc_ref[pl.ds(pl.multiple_of(n_qb * QT, QT), QT), :] = jnp.full((QT, QT), -jnp.inf, F32)

    def count(cand, strict=False):
        def body(pb, c):
            for sub in range(2):
                qs = pl.multiple_of((2 * pb + sub) * QT, QT)
                blk = sc_ref[pl.ds(qs, QT), :]
                hit = jnp.where(blk > cand if strict else blk >= cand, 1.0, 0.0)
                c = c + jnp.sum(hit.reshape(QT // 8, 8, QT), axis=0)
            return c
        c = lax.fori_loop(0, n_pairs, body, jnp.zeros((8, QT), F32))
        return jnp.sum(c, axis=0, keepdims=True)

    count_ge = count

    def search_step(c):
        lo, hi, c_lo, active = c
        mid = 0.5 * lo + 0.5 * hi
        cnt = count_ge(mid)
        up = (active > 0.0) & (cnt >= top_k)
        down = (active > 0.0) & (cnt < top_k)
        stuck = (mid <= lo) | (mid >= hi)
        lo, c_lo, hi = jnp.where(up, mid, lo), jnp.where(up, cnt, c_lo), jnp.where(down, mid, hi)
        active = jnp.where((c_lo > top_k) & jnp.logical_not(stuck), active, 0.0)
        return lo, hi, c_lo, active

    n_adm = limit.astype(F32)
    searching = n_adm > top_k
    zero = jnp.zeros((1, QT), F32)
    c_ge0, c_gt0 = count(zero), count(zero, strict=True)
    above = searching & (c_gt0 >= top_k)
    at_zero = searching & (c_gt0 < top_k) & (c_ge0 >= top_k)
    below = searching & (c_ge0 < top_k)
    state = (jnp.where(above | at_zero, zero, s_min), jnp.where(below | at_zero, zero, s_max),
             jnp.where(above | at_zero, c_ge0, n_adm),
             jnp.where(above | below, 1.0, 0.0))
    state = lax.fori_loop(0, SEARCH_FIXED_STEPS, lambda i, c: search_step(c), state)
    lo, hi, c_lo, _, _ = lax.while_loop(
        lambda c: jnp.logical_and(jnp.max(c[3]) > 0.0, c[4] < 200),
        lambda c: search_step(search_step(c[:4])) + (c[4] + 1,),
        state + (jnp.int32(0),))

    def bias_block(qb, carry):
        qs = pl.multiple_of(qb * QT, QT)
        bias_ref[pl.ds(qs, QT), :] = jnp.where(sc_ref[pl.ds(qs, QT), :] >= lo, 0.0, NEG_BIG)
        return carry

    lax.fori_loop(0, n_qb, bias_block, 0)

    @pl.when(jnp.max(c_lo) > top_k)
    def _():
        tau = jnp.where(count_ge(hi) >= top_k, hi, lo)
        need = top_k - count(tau, strict=True)
        r = lax.broadcasted_iota(I32, (QT, QT), 0)
        c = lax.broadcasted_iota(I32, (QT, QT), 1)
        lower = jnp.where(r > c, 1.0, 0.0).astype(BF16)

        def tie_block(qb, seen):
            qs = pl.multiple_of(qb * QT, QT)
            blk = sc_ref[pl.ds(qs, QT), :]
            eq = blk == tau
            eqf = jnp.where(eq, 1.0, 0.0)
            rank = jnp.dot(lower, eqf.astype(BF16), preferred_element_type=F32) + seen
            sel = (blk > tau) | (eq & (rank < need))
            bias_ref[pl.ds(qs, QT), :] = jnp.where(sel, 0.0, NEG_BIG)
            return seen + jnp.sum(eqf, axis=0, keepdims=True)

        lax.fori_loop(0, n_qb, tie_block, jnp.zeros((1, QT), F32))

    def logits(kb, s_ref):
        ks = pl.multiple_of(kb * KB, KB)
        bias = bias_ref[pl.ds(ks, KB), :]
        for h in range(ATTN_HEADS):
            s_ref[h] = lax.dot_general(k_ref[pl.ds(ks, KB), head(h)], qh_ref[h], nt_dims,
                                       preferred_element_type=F32) + bias

    def softmax(s_ref, p_ref, m):
        ms, alphas = [], []
        for h in range(ATTN_HEADS):
            s = s_ref[h]
            m_old = m[h:h + 1]
            m_new = jnp.maximum(m_old, jnp.max(s, axis=0, keepdims=True))
            p_ref[h] = jnp.exp2(s - m_new).astype(BF16)
            ms.append(m_new)
            alphas.append(jnp.exp2(m_old - m_new))
        cat = lambda rows: jnp.concatenate(rows, axis=0)
        return cat(ms), cat(alphas)

    ones_rows = jnp.ones((ACC_ROWS - HEAD_DIM, KB), BF16)

    def values(kb, p_ref, alpha):
        for h in range(ATTN_HEADS):
            lhs = jnp.concatenate([vt_ref[kb, head(h), :], ones_rows], axis=0)
            acc_ref[h] = acc_ref[h] * alpha[h:h + 1] + jnp.dot(lhs, p_ref[h],
                                                              preferred_element_type=F32)

    acc_ref[...] = jnp.zeros(acc_ref.shape, F32)
    pb_ref[...] = jnp.zeros(pb_ref.shape, BF16)
    logits(0, sa_ref)

    def attn_pair(j, carry):
        m, alpha_b = carry
        kb0 = 2 * j
        logits(kb0 + 1, sb_ref)
        m, alpha_a = softmax(sa_ref, pa_ref, m)
        values(jnp.maximum(kb0 - 1, 0), pb_ref, alpha_b)
        logits(jnp.minimum(kb0 + 2, n_kb - 1), sa_ref)
        m, alpha_b = softmax(sb_ref, pb_ref, m)
        values(kb0, pa_ref, alpha_a)
        return m, alpha_b

    heads_qt = (ATTN_HEADS, QT)
    _, alpha_b = lax.fori_loop(0, n_qb, attn_pair,
                               (jnp.full(heads_qt, NEG_BIG, F32), jnp.ones(heads_qt, F32)))
    values(n_kb - 1, pb_ref, alpha_b)
    for j in range(ATTN_HEADS // 2):
        pair = jnp.concatenate(
            [acc_ref[h, :HEAD_DIM, :] / acc_ref[h, HEAD_DIM:HEAD_DIM + 1, :] for h in (2 * j, 2 * j + 1)],
            axis=0)
        o_ref[:, j * LANES:(j + 1) * LANES] = pair.T.astype(BF16)


def _dsa(q, qi, wi_t, ki, k, vt_blk, B, S):
    nq = S // QT
    return pl.pallas_call(
        _dsa_kernel,
        grid=(B, nq),
        in_specs=[
            pl.BlockSpec((QT, ATTN_WIDTH), lambda b, t: (b * nq + t, 0)),
            pl.BlockSpec((QT, IDX_HEADS * IDX_DIM), lambda b, t: (b * nq + t, 0)),
            pl.BlockSpec((IDX_HEADS, QT), lambda b, t: (0, b * nq + t)),
            pl.BlockSpec((S, IDX_DIM), lambda b, t: (b, 0)),
            pl.BlockSpec((S, ATTN_WIDTH), lambda b, t: (b, 0)),
            pl.BlockSpec((S // KB, ATTN_WIDTH, KB), lambda b, t: (b, 0, 0)),
        ],
        out_specs=pl.BlockSpec((QT, ATTN_WIDTH), lambda b, t: (b * nq + t, 0)),
        out_shape=jax.ShapeDtypeStruct((B * S, ATTN_WIDTH), BF16),
        scratch_shapes=[pltpu.VMEM((S, QT), F32), pltpu.VMEM((S, QT), F32),
                        pltpu.VMEM((ATTN_HEADS, ACC_ROWS, QT), F32),
                        pltpu.VMEM((ATTN_HEADS, KB, QT), F32), pltpu.VMEM((ATTN_HEADS, KB, QT), F32),
                        pltpu.VMEM((ATTN_HEADS, KB, QT), BF16), pltpu.VMEM((ATTN_HEADS, KB, QT), BF16),
                        pltpu.VMEM((ATTN_HEADS, QT, HEAD_DIM), BF16),
                        pltpu.VMEM((IDX_HEADS, QT, IDX_DIM), BF16)],
        compiler_params=_cparams(("parallel", "arbitrary")),
        name="dsa",
    )(q, qi, wi_t, ki, k, vt_blk)


def _conv_kernel(u_ref, halo_ref, w_ref, b_ref, lg_ref, lb_ref, o_ref, buf_ref):
    i = pl.program_id(1)
    tc = u_ref.shape[1]
    halo = halo_ref[0]
    buf_ref[0, 0:HALO, :] = jnp.where(i > 0, halo, jnp.zeros_like(halo))
    buf_ref[0, HALO:HALO + tc, :] = u_ref[0]
    span = HALO + tc - SUBLANES
    for r in range(1, SUBLANES):
        buf_ref[r, 0:span, :] = buf_ref[0, r:r + span, :]
    off = HALO - (CONV_WIDTH - 1)
    rows = 64
    for rb in range(tc // rows):
        acc = jnp.zeros((rows, CONV_CH), F32) + b_ref[...]
        for j in range(CONV_WIDTH):
            r = (off + j) % SUBLANES
            s = rb * rows + off + j - r
            acc = acc + buf_ref[r, s:s + rows, :] * w_ref[j:j + 1, :]
        mu = jnp.mean(acc, axis=-1, keepdims=True)
        d = acc - mu
        var = jnp.mean(d * d, axis=-1, keepdims=True)
        y = d * lax.rsqrt(var + EPS) * lg_ref[...] + lb_ref[...]
        o_ref[0, rb * rows:(rb + 1) * rows, :] = (y * jax.nn.sigmoid(y)).astype(BF16)


def _conv(u, conv_w, conv_b, ln_g, ln_b):
    B, S, C = u.shape
    tc = TC_CONV
    full = lambda a: pl.BlockSpec(a.shape, lambda b, i: (0,) * a.ndim)
    return pl.pallas_call(
        _conv_kernel,
        grid=(B, S // tc),
        in_specs=[
            pl.BlockSpec((1, tc, C), lambda b, i: (b, i, 0)),
            pl.BlockSpec((1, HALO, C), lambda b, i: (b, jnp.maximum(i * (tc // HALO) - 1, 0), 0)),
            full(conv_w), full(conv_b), full(ln_g), full(ln_b),
        ],
        out_specs=pl.BlockSpec((1, tc, C), lambda b, i: (b, i, 0)),
        out_shape=jax.ShapeDtypeStruct((B, S, C), BF16),
        scratch_shapes=[pltpu.VMEM((SUBLANES, HALO + tc, C), F32)],
        compiler_params=_cparams(("parallel", "parallel")),
        name="conv",
    )(u, u, conv_w, conv_b, ln_g, ln_b)


def _memkv_kernel(mem_ref, g_ref, w_ref, o_ref):
    h = _rms(mem_ref[0], g_ref[...]).astype(BF16)
    o_ref[0] = jnp.dot(h, w_ref[...], preferred_element_type=F32).astype(BF16)


def _mem_kv(mem, g, w_kv):
    B, M, D = mem.shape
    return pl.pallas_call(
        _memkv_kernel,
        grid=(B,),
        in_specs=[pl.BlockSpec((1, M, D), lambda b: (b, 0, 0)),
                  pl.BlockSpec(g.shape, lambda b: (0, 0)),
                  pl.BlockSpec(w_kv.shape, lambda b: (0, 0))],
        out_specs=pl.BlockSpec((1, M, 2 * D), lambda b: (b, 0, 0)),
        out_shape=jax.ShapeDtypeStruct((B, M, 2 * D), BF16),
        compiler_params=_cparams(("parallel",)),
        name="mem_kv",
    )(mem, g, w_kv)


def _mix_kernel(x_ref, a_ref, c_ref, gs_ref, kv_ref, woa_ref, wco_ref, wout_ref, gx_ref,
                wq_ref, wox_ref, o_ref):
    dot = functools.partial(jnp.dot, preferred_element_type=F32)
    y_attn = dot(a_ref[0], woa_ref[...])
    y_conv = dot(c_ref[0], wco_ref[...])
    gs = gs_ref[0]
    merged = gs[:, :D_MODEL] * y_attn + gs[:, D_MODEL:] * y_conv
    x1 = x_ref[0] + dot(merged.astype(BF16), wout_ref[...])

    q = (dot(_rms(x1, gx_ref[...]).astype(BF16), wq_ref[...]) * (X_HEAD_DIM ** -0.5)).astype(BF16)
    heads = []
    for h in range(X_HEADS):
        lo = h * X_HEAD_DIM
        kh = kv_ref[0, :, lo:lo + X_HEAD_DIM]
        vh = kv_ref[0, :, D_MODEL + lo:D_MODEL + lo + X_HEAD_DIM]
        s = lax.dot_general(q[:, lo:lo + X_HEAD_DIM], kh, (((1,), (1,)), ((), ())),
                            preferred_element_type=F32)
        p = jnp.exp(s - jnp.max(s, axis=-1, keepdims=True))
        p = p / jnp.sum(p, axis=-1, keepdims=True)
        heads.append(dot(p.astype(BF16), vh))
    o = jnp.concatenate(heads, axis=-1)
    o_ref[0] = x1 + dot(o.astype(BF16), wox_ref[...])


def _mix_xattn(x, attn, uc, gs, kv, woa, wco, wout, gx, wq, wox):
    B, S, D = x.shape
    tm = TM_MIX
    row = lambda w: pl.BlockSpec((1, tm, w), lambda b, i: (b, i, 0))
    full = lambda a: pl.BlockSpec(a.shape, lambda b, i: (0,) * a.ndim)
    return pl.pallas_call(
        _mix_kernel,
        grid=(B, S // tm),
        in_specs=[row(D), row(ATTN_WIDTH), row(CONV_CH), row(2 * D),
                  pl.BlockSpec((1,) + kv.shape[1:], lambda b, i: (b, 0, 0)),
                  full(woa), full(wco), full(wout), full(gx), full(wq), full(wox)],
        out_specs=row(D),
        out_shape=jax.ShapeDtypeStruct((B, S, D), F32),
        compiler_params=_cparams(("parallel", "parallel")),
        name="mix_xattn",
    )(x, attn, uc, gs, kv, woa, wco, wout, gx, wq, wox)


def _moe_kernel(x_ref, gm_ref, wr_ref, br_ref, wg_ref, wu_ref, wd_ref, gf_ref, o_ref,
                hn_ref, comb_ref, y_ref):
    g = pl.program_id(1)
    tt = x_ref.shape[0]
    lane = lax.broadcasted_iota(I32, (tt, LANES), 1)

    @pl.when(g == 0)
    def _():
        hn = _rms(x_ref[...], gm_ref[...]).astype(BF16)
        hn_ref[...] = hn
        logits = jnp.dot(hn, wr_ref[...], preferred_element_type=F32) + br_ref[...]
        gl = jnp.where(lane < N_GROUPS, logits, -jnp.inf)
        gmax = jnp.max(gl, axis=-1, keepdims=True)
        g_sel = jnp.min(jnp.where(gl == gmax, lane, LANES), axis=-1, keepdims=True)
        g_gate = 1.0 / jnp.sum(jnp.exp(gl - gmax), axis=-1, keepdims=True)
        e_lo = N_GROUPS + g_sel * EXPERTS_PER_GROUP
        el = jnp.where((lane >= e_lo) & (lane < e_lo + EXPERTS_PER_GROUP), logits, -jnp.inf)
        top1 = jnp.max(el, axis=-1, keepdims=True)
        i1 = jnp.min(jnp.where(el == top1, lane, LANES), axis=-1, keepdims=True)
        el2 = jnp.where(lane == i1, -jnp.inf, el)
        top2 = jnp.max(el2, axis=-1, keepdims=True)
        i2 = jnp.min(jnp.where(el2 == top2, lane, LANES), axis=-1, keepdims=True)
        e2 = jnp.exp(top2 - top1)
        w1 = g_gate / (1.0 + e2)
        w2 = g_gate * e2 / (1.0 + e2)
        comb_ref[...] = jnp.where(lane == i1, w1, 0.0) + jnp.where(lane == i2, w2, 0.0)
        y_ref[...] = jnp.zeros_like(y_ref)

    hn = hn_ref[...]
    comb = comb_ref[...]
    y = y_ref[...]
    for e in range(EXPERTS_PER_GROUP):
        col = N_GROUPS + g * EXPERTS_PER_GROUP + e
        c = jnp.sum(jnp.where(lane == col, comb, 0.0), axis=-1, keepdims=True)
        gate = jnp.dot(hn, wg_ref[0, e], preferred_element_type=F32)
        up = jnp.dot(hn, wu_ref[0, e], preferred_element_type=F32)
        act = (gate * jax.nn.sigmoid(gate)) * up * c
        y = y + jnp.dot(act.astype(BF16), wd_ref[0, e], preferred_element_type=F32)
    y_ref[...] = y

    @pl.when(g == N_GROUPS - 1)
    def _():
        o_ref[...] = _rms(x_ref[...] + y_ref[...], gf_ref[...])


def _moe(x2d, gm, wr, br, wg, wu, wd, gf):
    T, D = x2d.shape
    tt = TT_MOE
    full = lambda a: pl.BlockSpec(a.shape, lambda i, g: (0,) * a.ndim)
    return pl.pallas_call(
        _moe_kernel,
        grid=(T // tt, N_GROUPS),
        in_specs=[pl.BlockSpec((tt, D), lambda i, g: (i, 0)), full(gm), full(wr), full(br),
                  pl.BlockSpec((1, EXPERTS_PER_GROUP, D, EXPERT_FF), lambda i, g: (g, 0, 0, 0)),
                  pl.BlockSpec((1, EXPERTS_PER_GROUP, D, EXPERT_FF), lambda i, g: (g, 0, 0, 0)),
                  pl.BlockSpec((1, EXPERTS_PER_GROUP, EXPERT_FF, D), lambda i, g: (g, 0, 0, 0)),
                  full(gf)],
        out_specs=pl.BlockSpec((tt, D), lambda i, g: (i, 0)),
        out_shape=jax.ShapeDtypeStruct((T, D), F32),
        scratch_shapes=[pltpu.VMEM((tt, D), BF16), pltpu.VMEM((tt, LANES), F32),
                        pltpu.VMEM((tt, D), F32)],
        compiler_params=_cparams(("parallel", "arbitrary")),
        name="moe",
    )(x2d, gm, wr, br, wg, wu, wd, gf)


def kernel(x, mem, positions, norm_mix_g, w_in, w_o_attn, conv_w, conv_b, conv_ln_g, conv_ln_b,
           w_conv_out, w_out, norm_x_g, norm_mem_g, w_q_x, w_kv_x, w_o_x, norm_moe_g,
           w_router_group, b_router_group, w_router_expert, b_router_expert,
           w_exp_gate, w_exp_up, w_exp_down, norm_final_g):
    B, S, D = x.shape
    T = B * S
    depth = norm_mix_g.shape[0]

    inv_freq = ROPE_THETA ** (-jnp.arange(0, HEAD_DIM, 2, dtype=F32) / HEAD_DIM)
    ang = positions.astype(F32)[..., None] * inv_freq
    cos, sin = jnp.cos(ang), jnp.sin(ang)
    cos128 = jnp.tile(cos, (1, 1, 4)).reshape(T, LANES)
    sin128 = jnp.tile(jnp.concatenate([-sin, sin], axis=-1), (1, 1, 2)).reshape(T, LANES)

    row = lambda a: a.reshape(1, -1)
    for l in range(depth):
        w = w_in[l]
        wm = jnp.concatenate([w[:, 0:512], w[:, 512:1024], w[:, 1536:2048],
                              w[:, 2120:3144], w[:, 3144:5192]], axis=1).astype(BF16)
        ws = jnp.pad(w[:, 2048:2112], ((0, 0), (0, LANES - IDX_DIM))).astype(BF16)
        wvt = w[:, 1024:1536].T.astype(BF16)
        wwt = w[:, 2112:2120].T.astype(BF16)
        q, k, qi, vt_blk, ki, wi_t, u, gs = _in_proj(x.reshape(T, D), row(norm_mix_g[l]), wm, ws, wvt, wwt,
                                                    cos128, sin128)
        attn = _dsa(q, qi, wi_t, ki, k, vt_blk, B, S).reshape(B, S, ATTN_WIDTH)

        uc = _conv(u.reshape(B, S, CONV_CH), jnp.pad(conv_w[l], ((0, 1), (0, 0))), row(conv_b[l]),
                   row(conv_ln_g[l]), row(conv_ln_b[l]))
        kv = _mem_kv(mem, row(norm_mem_g[l]), w_kv_x[l].astype(BF16))
        x = _mix_xattn(x, attn, uc, gs.reshape(B, S, 2 * D), kv,
                       w_o_attn[l].astype(BF16), w_conv_out[l].astype(BF16), w_out[l].astype(BF16),
                       row(norm_x_g[l]), w_q_x[l].astype(BF16), w_o_x[l].astype(BF16))

        wr = jnp.concatenate([w_router_group[l],
                              w_router_expert[l].transpose(1, 0, 2).reshape(D, N_EXPERTS)], axis=1)
        wr = jnp.pad(wr, ((0, 0), (0, LANES - N_GROUPS - N_EXPERTS))).astype(BF16)
        br = jnp.pad(jnp.concatenate([b_router_group[l], b_router_expert[l].reshape(-1)]),
                     (0, LANES - N_GROUPS - N_EXPERTS)).reshape(1, LANES)
        last = l == depth - 1
        gf = row(norm_final_g) if last else None
        assert last, "the final norm is fused into the last layer's MoE kernel"
        x = _moe(x.reshape(T, D), row(norm_moe_g[l]), wr, br, w_exp_gate[l].astype(BF16),
                 w_exp_up[l].astype(BF16), w_exp_down[l].astype(BF16), gf).reshape(B, S, D)
    return x
```

```python
import functools

import jax
import jax.numpy as jnp
from jax import lax
from jax.experimental import pallas as pl
from jax.experimental.pallas import tpu as pltpu

F32 = jnp.float32
BF16 = jnp.bfloat16
I32 = jnp.int32

D_MODEL = 1024
CHUNK = 64
ROPE_THETA = 10000.0
EPS = 1e-6
ATTN_HEADS = 8
HEAD_DIM = 64
ATTN_WIDTH = ATTN_HEADS * HEAD_DIM
IDX_HEADS = 8
IDX_DIM = 64
MAX_TOPK = 256
CONV_CH = D_MODEL // 2
CONV_WIDTH = 31
X_HEADS = 4
X_HEAD_DIM = D_MODEL // X_HEADS
N_GROUPS = 4
EXPERTS_PER_GROUP = 8
N_EXPERTS = N_GROUPS * EXPERTS_PER_GROUP
EXPERT_FF = D_MODEL // 4

LANES = 128
SUBLANES = 8
VMEM_LIMIT = 56 * 1024 * 1024
NEG_BIG = -1e30

TM_IN = 256
QT = 256
KB = 128
SEARCH_FIXED_STEPS = 14
ACC_ROWS = HEAD_DIM + 16
LOG2E = 1.4426950408889634
TC_CONV = 256
HALO = 32
TM_MIX = 512
TT_MOE = 1024
MOE_WINDOW = 256


def _cparams(sem):
    return pltpu.CompilerParams(dimension_semantics=sem, vmem_limit_bytes=VMEM_LIMIT)


def _rms(x, g):
    return x * lax.rsqrt(jnp.mean(x * x, axis=-1, keepdims=True) + EPS) * g


def _inproj_kernel(x_ref, g_ref, wm_ref, ws_ref, wvt_ref, wwt_ref, cos_ref, sin_ref,
                   q_ref, k_ref, qi_ref, vt_ref, ki_ref, wit_ref, u_ref, gs_ref):
    tm = x_ref.shape[0]
    nt_dims = (((1,), (1,)), ((), ()))
    h = _rms(x_ref[...], g_ref[...]).astype(BF16)
    cos = cos_ref[...]
    sin = sin_ref[...]
    lane = lax.broadcasted_iota(I32, (tm, LANES), 1)
    first_half = (lane % HEAD_DIM) < (HEAD_DIM // 2)

    def rope(y):
        rot = jnp.where(first_half, pltpu.roll(y, LANES - HEAD_DIM // 2, 1),
                        pltpu.roll(y, HEAD_DIM // 2, 1))
        return y * cos + rot * sin

    def proj(c0, w):
        return jnp.dot(h, wm_ref[:, c0:c0 + w], preferred_element_type=F32)

    for ref, c0, scale in ((q_ref, 0, HEAD_DIM ** -0.5 * LOG2E), (k_ref, 512, None), (qi_ref, 1024, None)):
        y = proj(c0, ATTN_WIDTH)
        for c in range(ATTN_WIDTH // LANES):
            r = rope(y[:, c * LANES:(c + 1) * LANES])
            if scale is not None:
                r = r * scale
            ref[:, c * LANES:(c + 1) * LANES] = r.astype(BF16)
    v_t = lax.dot_general(wvt_ref[...], h, nt_dims, preferred_element_type=F32)
    for c in range(tm // KB):
        vt_ref[c] = v_t[:, c * KB:(c + 1) * KB].astype(BF16)
    wit_ref[...] = lax.dot_general(wwt_ref[...], h, nt_dims, preferred_element_type=F32)
    glu = proj(1536, 2 * CONV_CH)
    u_ref[...] = glu[:, :CONV_CH] * jax.nn.sigmoid(glu[:, CONV_CH:])
    for c in range(4):
        gs_ref[:, c * 512:(c + 1) * 512] = jax.nn.sigmoid(proj(2560 + c * 512, 512))
    ys = jnp.dot(h, ws_ref[...], preferred_element_type=F32)
    ki_ref[...] = rope(ys)[:, :IDX_DIM].astype(BF16)


def _in_proj(x2d, g, wm, ws, wvt, wwt, cos128, sin128):
    T = x2d.shape[0]
    tm = TM_IN
    row = lambda w: pl.BlockSpec((tm, w), lambda i: (i, 0))
    full = lambda a: pl.BlockSpec(a.shape, lambda i: (0,) * a.ndim)
    return pl.pallas_call(
        _inproj_kernel,
        grid=(T // tm,),
        in_specs=[row(D_MODEL), full(g), full(wm), full(ws), full(wvt), full(wwt), row(LANES), row(LANES)],
        out_specs=[row(512), row(512), row(512),
                   pl.BlockSpec((tm // KB, ATTN_WIDTH, KB), lambda i: (i, 0, 0)),
                   row(IDX_DIM), pl.BlockSpec((IDX_HEADS, tm), lambda i: (0, i)),
                   row(CONV_CH), row(2 * D_MODEL)],
        out_shape=[jax.ShapeDtypeStruct((T, 512), BF16)] * 3
        + [jax.ShapeDtypeStruct((T // KB, ATTN_WIDTH, KB), BF16),
           jax.ShapeDtypeStruct((T, IDX_DIM), BF16), jax.ShapeDtypeStruct((IDX_HEADS, T), F32),
           jax.ShapeDtypeStruct((T, CONV_CH), F32), jax.ShapeDtypeStruct((T, 2 * D_MODEL), F32)],
        compiler_params=_cparams(("parallel",)),
        name="in_proj",
    )(x2d, g, wm, ws, wvt, wwt, cos128, sin128)


def _dsa_kernel(q_ref, qi_ref, wi_ref, ki_ref, k_ref, vt_ref, o_ref,
                sc_ref, bias_ref, acc_ref, sa_ref, sb_ref, pa_ref, pb_ref, qh_ref, qih_ref):
    assert QT == 2 * KB
    t = pl.program_id(1)
    n_qb = t + 1
    n_kb = 2 * n_qb
    scale_idx = (IDX_DIM ** -0.5) * (IDX_HEADS ** -0.5)
    nt_dims = (((1,), (1,)), ((), ()))
    top_k = float(MAX_TOPK)
    head = lambda h: slice(h * HEAD_DIM, (h + 1) * HEAD_DIM)

    qcol = lax.broadcasted_iota(I32, (1, QT), 1) + t * QT
    limit = (qcol // CHUNK + 1) * CHUNK
    w_idx = wi_ref[...]
    for h in range(ATTN_HEADS):
        qh_ref[h] = q_ref[:, head(h)]
        qih_ref[h] = qi_ref[:, head(h)]

    def score_rows(ks, masked):
        ki_blk = ki_ref[pl.ds(ks, KB), :]
        acc = jnp.zeros((KB, QT), F32)
        for h in range(IDX_HEADS):
            rel = lax.dot_general(ki_blk, qih_ref[h], nt_dims, preferred_element_type=F32)
            acc = acc + jnp.maximum(rel, 0.0) * w_idx[h:h + 1, :]
        score = acc * scale_idx
        if masked:
            adm = lax.broadcasted_iota(I32, (KB, QT), 0) + ks < limit
            lo_src = jnp.where(adm, score, jnp.inf)
            score = jnp.where(adm, score, -jnp.inf)
        else:
            lo_src = score
        sc_ref[pl.ds(ks, KB), :] = score
        fold = lambda a, op: functools.reduce(op, [a[r * 8:(r + 1) * 8] for r in range(KB // 8)])
        return fold(lo_src, jnp.minimum), fold(score, jnp.maximum)

    def score_block(qb, carry, masked=False):
        mn, mx = carry
        for sub in range(QT // KB):
            ks = pl.multiple_of(qb * QT + sub * KB, KB)
            bmn, bmx = score_rows(ks, masked)
            mn, mx = jnp.minimum(mn, bmn), jnp.maximum(mx, bmx)
        return mn, mx

    init = (jnp.full((8, QT), jnp.inf, F32), jnp.full((8, QT), -jnp.inf, F32))
    mn8, mx8 = score_block(t, lax.fori_loop(0, t, score_block, init), masked=True)
    s_min = jnp.min(mn8, axis=0, keepdims=True)
    s_max = jnp.max(mx8, axis=0, keepdims=True)

    n_pairs = (n_qb + 1) // 2

    @pl.when(n_qb % 2 == 1)
    def _():
        sc_ref[pl.ds(pl.multiple_of(n_qb * QT, QT), QT), :] = jnp.full((QT, QT), -jnp.inf, F32)

    def count(cand, strict=False):
        def body(pb, c):
            for sub in range(2):
                qs = pl.multiple_of((2 * pb + sub) * QT, QT)
                blk = sc_ref[pl.ds(qs, QT), :]
                hit = jnp.where(blk > cand if strict else blk >= cand, 1.0, 0.0)
                c = c + jnp.sum(hit.reshape(QT // 8, 8, QT), axis=0)
            return c
        c = lax.fori_loop(0, n_pairs, body, jnp.zeros((8, QT), F32))
        return jnp.sum(c, axis=0, keepdims=True)

    count_ge = count

    def search_step(c):
        lo, hi, c_lo, active = c
        mid = 0.5 * lo + 0.5 * hi
        cnt = count_ge(mid)
        up = (active > 0.0) & (cnt >= top_k)
        down = (active > 0.0) & (cnt < top_k)
        stuck = (mid <= lo) | (mid >= hi)
        lo, c_lo, hi = jnp.where(up, mid, lo), jnp.where(up, cnt, c_lo), jnp.where(down, mid, hi)
        active = jnp.where((c_lo > top_k) & jnp.logical_not(stuck), active, 0.0)
        return lo, hi, c_lo, active

    n_adm = limit.astype(F32)
    searching = n_adm > top_k
    zero = jnp.zeros((1, QT), F32)
    c_ge0, c_gt0 = count(zero), count(zero, strict=True)
    above = searching & (c_gt0 >= top_k)
    at_zero = searching & (c_gt0 < top_k) & (c_ge0 >= top_k)
    below = searching & (c_ge0 < top_k)
    state = (jnp.where(above | at_zero, zero, s_min), jnp.where(below | at_zero, zero, s_max),
             jnp.where(above | at_zero, c_ge0, n_adm),
             jnp.where(above | below, 1.0, 0.0))
    state = lax.fori_loop(0, SEARCH_FIXED_STEPS, lambda i, c: search_step(c), state)
    lo, hi, c_lo, _, _ = lax.while_loop(
        lambda c: jnp.logical_and(jnp.max(c[3]) > 0.0, c[4] < 200),
        lambda c: search_step(search_step(c[:4])) + (c[4] + 1,),
        state + (jnp.int32(0),))

    def bias_block(qb, carry):
        qs = pl.multiple_of(qb * QT, QT)
        bias_ref[pl.ds(qs, QT), :] = jnp.where(sc_ref[pl.ds(qs, QT), :] >= lo, 0.0, NEG_BIG)
        return carry

    lax.fori_loop(0, n_qb, bias_block, 0)

    @pl.when(jnp.max(c_lo) > top_k)
    def _():
        tau = jnp.where(count_ge(hi) >= top_k, hi, lo)
        need = top_k - count(tau, strict=True)
        r = lax.broadcasted_iota(I32, (QT, QT), 0)
        c = lax.broadcasted_iota(I32, (QT, QT), 1)
        lower = jnp.where(r > c, 1.0, 0.0).astype(BF16)

        def tie_block(qb, seen):
            qs = pl.multiple_of(qb * QT, QT)
            blk = sc_ref[pl.ds(qs, QT), :]
            eq = blk == tau
            eqf = jnp.where(eq, 1.0, 0.0)
            rank = jnp.dot(lower, eqf.astype(BF16), preferred_element_type=F32) + seen
            sel = (blk > tau) | (eq & (rank < need))
            bias_ref[pl.ds(qs, QT), :] = jnp.where(sel, 0.0, NEG_BIG)
            return seen + jnp.sum(eqf, axis=0, keepdims=True)

        lax.fori_loop(0, n_qb, tie_block, jnp.zeros((1, QT), F32))

    def logits(kb, s_ref):
        ks = pl.multiple_of(kb * KB, KB)
        bias = bias_ref[pl.ds(ks, KB), :]
        for h in range(ATTN_HEADS):
            s_ref[h] = lax.dot_general(k_ref[pl.ds(ks, KB), head(h)], qh_ref[h], nt_dims,
                                       preferred_element_type=F32) + bias

    def softmax(s_ref, p_ref, m):
        ms, alphas = [], []
        for h in range(ATTN_HEADS):
            s = s_ref[h]
            m_old = m[h:h + 1]
            m_new = jnp.maximum(m_old, jnp.max(s, axis=0, keepdims=True))
            p_ref[h] = jnp.exp2(s - m_new).astype(BF16)
            ms.append(m_new)
            alphas.append(jnp.exp2(m_old - m_new))
        cat = lambda rows: jnp.concatenate(rows, axis=0)
        return cat(ms), cat(alphas)

    ones_rows = jnp.ones((ACC_ROWS - HEAD_DIM, KB), BF16)

    def values(kb, p_ref, alpha):
        for h in range(ATTN_HEADS):
            lhs = jnp.concatenate([vt_ref[kb, head(h), :], ones_rows], axis=0)
            acc_ref[h] = acc_ref[h] * alpha[h:h + 1] + jnp.dot(lhs, p_ref[h],
                                                              preferred_element_type=F32)

    acc_ref[...] = jnp.zeros(acc_ref.shape, F32)
    pb_ref[...] = jnp.zeros(pb_ref.shape, BF16)
    logits(0, sa_ref)

    def attn_pair(j, carry):
        m, alpha_b = carry
        kb0 = 2 * j
        logits(kb0 + 1, sb_ref)
        m, alpha_a = softmax(sa_ref, pa_ref, m)
        values(jnp.maximum(kb0 - 1, 0), pb_ref, alpha_b)
        logits(jnp.minimum(kb0 + 2, n_kb - 1), sa_ref)
        m, alpha_b = softmax(sb_ref, pb_ref, m)
        values(kb0, pa_ref, alpha_a)
        return m, alpha_b

    heads_qt = (ATTN_HEADS, QT)
    _, alpha_b = lax.fori_loop(0, n_qb, attn_pair,
                               (jnp.full(heads_qt, NEG_BIG, F32), jnp.ones(heads_qt, F32)))
    values(n_kb - 1, pb_ref, alpha_b)
    for j in range(ATTN_HEADS // 2):
        pair = jnp.concatenate(
            [acc_ref[h, :HEAD_DIM, :] / acc_ref[h, HEAD_DIM:HEAD_DIM + 1, :] for h in (2 * j, 2 * j + 1)],
            axis=0)
        o_ref[:, j * LANES:(j + 1) * LANES] = pair.T.astype(BF16)


def _dsa(q, qi, wi_t, ki, k, vt_blk, B, S):
    nq = S // QT
    return pl.pallas_call(
        _dsa_kernel,
        grid=(B, nq),
        in_specs=[
            pl.BlockSpec((QT, ATTN_WIDTH), lambda b, t: (b * nq + t, 0)),
            pl.BlockSpec((QT, IDX_HEADS * IDX_DIM), lambda b, t: (b * nq + t, 0)),
            pl.BlockSpec((IDX_HEADS, QT), lambda b, t: (0, b * nq + t)),
            pl.BlockSpec((S, IDX_DIM), lambda b, t: (b, 0)),
            pl.BlockSpec((S, ATTN_WIDTH), lambda b, t: (b, 0)),
            pl.BlockSpec((S // KB, ATTN_WIDTH, KB), lambda b, t: (b, 0, 0)),
        ],
        out_specs=pl.BlockSpec((QT, ATTN_WIDTH), lambda b, t: (b * nq + t, 0)),
        out_shape=jax.ShapeDtypeStruct((B * S, ATTN_WIDTH), BF16),
        scratch_shapes=[pltpu.VMEM((S, QT), F32), pltpu.VMEM((S, QT), F32),
                        pltpu.VMEM((ATTN_HEADS, ACC_ROWS, QT), F32),
                        pltpu.VMEM((ATTN_HEADS, KB, QT), F32), pltpu.VMEM((ATTN_HEADS, KB, QT), F32),
                        pltpu.VMEM((ATTN_HEADS, KB, QT), BF16), pltpu.VMEM((ATTN_HEADS, KB, QT), BF16),
                        pltpu.VMEM((ATTN_HEADS, QT, HEAD_DIM), BF16),
                        pltpu.VMEM((IDX_HEADS, QT, IDX_DIM), BF16)],
        compiler_params=_cparams(("parallel", "arbitrary")),
        name="dsa",
    )(q, qi, wi_t, ki, k, vt_blk)


def _conv_kernel(u_ref, halo_ref, w_ref, b_ref, lg_ref, lb_ref, o_ref, buf_ref):
    i = pl.program_id(1)
    tc = u_ref.shape[1]
    halo = halo_ref[0]
    buf_ref[0, 0:HALO, :] = jnp.where(i > 0, halo, jnp.zeros_like(halo))
    buf_ref[0, HALO:HALO + tc, :] = u_ref[0]
    span = HALO + tc - SUBLANES
    for r in range(1, SUBLANES):
        buf_ref[r, 0:span, :] = buf_ref[0, r:r + span, :]
    off = HALO - (CONV_WIDTH - 1)
    rows = 64
    for rb in range(tc // rows):
        acc = jnp.zeros((rows, CONV_CH), F32) + b_ref[...]
        for j in range(CONV_WIDTH):
            r = (off + j) % SUBLANES
            s = rb * rows + off + j - r
            acc = acc + buf_ref[r, s:s + rows, :] * w_ref[j:j + 1, :]
        mu = jnp.mean(acc, axis=-1, keepdims=True)
        d = acc - mu
        var = jnp.mean(d * d, axis=-1, keepdims=True)
        y = d * lax.rsqrt(var + EPS) * lg_ref[...] + lb_ref[...]
        o_ref[0, rb * rows:(rb + 1) * rows, :] = (y * jax.nn.sigmoid(y)).astype(BF16)


def _conv(u, conv_w, conv_b, ln_g, ln_b):
    B, S, C = u.shape
    tc = TC_CONV
    full = lambda a: pl.BlockSpec(a.shape, lambda b, i: (0,) * a.ndim)
    return pl.pallas_call(
        _conv_kernel,
        grid=(B, S // tc),
        in_specs=[
            pl.BlockSpec((1, tc, C), lambda b, i: (b, i, 0)),
            pl.BlockSpec((1, HALO, C), lambda b, i: (b, jnp.maximum(i * (tc // HALO) - 1, 0), 0)),
            full(conv_w), full(conv_b), full(ln_g), full(ln_b),
        ],
        out_specs=pl.BlockSpec((1, tc, C), lambda b, i: (b, i, 0)),
        out_shape=jax.ShapeDtypeStruct((B, S, C), BF16),
        scratch_shapes=[pltpu.VMEM((SUBLANES, HALO + tc, C), F32)],
        compiler_params=_cparams(("parallel", "parallel")),
        name="conv",
    )(u, u, conv_w, conv_b, ln_g, ln_b)


def _memkv_kernel(mem_ref, g_ref, w_ref, o_ref):
    h = _rms(mem_ref[0], g_ref[...]).astype(BF16)
    o_ref[0] = jnp.dot(h, w_ref[...], preferred_element_type=F32).astype(BF16)


def _mem_kv(mem, g, w_kv):
    B, M, D = mem.shape
    return pl.pallas_call(
        _memkv_kernel,
        grid=(B,),
        in_specs=[pl.BlockSpec((1, M, D), lambda b: (b, 0, 0)),
                  pl.BlockSpec(g.shape, lambda b: (0, 0)),
                  pl.BlockSpec(w_kv.shape, lambda b: (0, 0))],
        out_specs=pl.BlockSpec((1, M, 2 * D), lambda b: (b, 0, 0)),
        out_shape=jax.ShapeDtypeStruct((B, M, 2 * D), BF16),
        compiler_params=_cparams(("parallel",)),
        name="mem_kv",
    )(mem, g, w_kv)


def _mix_kernel(x_ref, a_ref, c_ref, gs_ref, kv_ref, woa_ref, wco_ref, wout_ref, gx_ref,
                wq_ref, wox_ref, o_ref):
    dot = functools.partial(jnp.dot, preferred_element_type=F32)
    y_attn = dot(a_ref[0], woa_ref[...])
    y_conv = dot(c_ref[0], wco_ref[...])
    gs = gs_ref[0]
    merged = gs[:, :D_MODEL] * y_attn + gs[:, D_MODEL:] * y_conv
    x1 = x_ref[0] + dot(merged.astype(BF16), wout_ref[...])

    q = (dot(_rms(x1, gx_ref[...]).astype(BF16), wq_ref[...]) * (X_HEAD_DIM ** -0.5)).astype(BF16)
    heads = []
    for h in range(X_HEADS):
        lo = h * X_HEAD_DIM
        kh = kv_ref[0, :, lo:lo + X_HEAD_DIM]
        vh = kv_ref[0, :, D_MODEL + lo:D_MODEL + lo + X_HEAD_DIM]
        s = lax.dot_general(q[:, lo:lo + X_HEAD_DIM], kh, (((1,), (1,)), ((), ())),
                            preferred_element_type=F32)
        p = jnp.exp(s - jnp.max(s, axis=-1, keepdims=True))
        p = p / jnp.sum(p, axis=-1, keepdims=True)
        heads.append(dot(p.astype(BF16), vh))
    o = jnp.concatenate(heads, axis=-1)
    o_ref[0] = x1 + dot(o.astype(BF16), wox_ref[...])


def _mix_xattn(x, attn, uc, gs, kv, woa, wco, wout, gx, wq, wox):
    B, S, D = x.shape
    tm = TM_MIX
    row = lambda w: pl.BlockSpec((1, tm, w), lambda b, i: (b, i, 0))
    full = lambda a: pl.BlockSpec(a.shape, lambda b, i: (0,) * a.ndim)
    return pl.pallas_call(
        _mix_kernel,
        grid=(B, S // tm),
        in_specs=[row(D), row(ATTN_WIDTH), row(CONV_CH), row(2 * D),
                  pl.BlockSpec((1,) + kv.shape[1:], lambda b, i: (b, 0, 0)),
                  full(woa), full(wco), full(wout), full(gx), full(wq), full(wox)],
        out_specs=row(D),
        out_shape=jax.ShapeDtypeStruct((B, S, D), F32),
        compiler_params=_cparams(("parallel", "parallel")),
        name="mix_xattn",
    )(x, attn, uc, gs, kv, woa, wco, wout, gx, wq, wox)


def _moe_kernel(x_ref, gm_ref, wr_ref, br_ref, wg_ref, wu_ref, wd_ref, gf_ref, o_ref,
                hn_ref, comb_ref, pos_ref, seg_ref):
    g = pl.program_id(1)
    tt = x_ref.shape[0]
    lane = lax.broadcasted_iota(I32, (tt, LANES), 1)

    @pl.when(g == 0)
    def _():
        hn = _rms(x_ref[...], gm_ref[...]).astype(BF16)
        logits = jnp.dot(hn, wr_ref[...], preferred_element_type=F32) + br_ref[...]
        gl = jnp.where(lane < N_GROUPS, logits, -jnp.inf)
        gmax = jnp.max(gl, axis=-1, keepdims=True)
        g_sel = jnp.min(jnp.where(gl == gmax, lane, LANES), axis=-1, keepdims=True)
        g_gate = 1.0 / jnp.sum(jnp.exp(gl - gmax), axis=-1, keepdims=True)
        e_lo = N_GROUPS + g_sel * EXPERTS_PER_GROUP
        el = jnp.where((lane >= e_lo) & (lane < e_lo + EXPERTS_PER_GROUP), logits, -jnp.inf)
        top1 = jnp.max(el, axis=-1, keepdims=True)
        i1 = jnp.min(jnp.where(el == top1, lane, LANES), axis=-1, keepdims=True)
        el2 = jnp.where(lane == i1, -jnp.inf, el)
        top2 = jnp.max(el2, axis=-1, keepdims=True)
        i2 = jnp.min(jnp.where(el2 == top2, lane, LANES), axis=-1, keepdims=True)
        e2 = jnp.exp(top2 - top1)
        w1 = g_gate / (1.0 + e2)
        w2 = g_gate * e2 / (1.0 + e2)
        comb = jnp.where(lane == i1, w1, 0.0) + jnp.where(lane == i2, w2, 0.0)

        member = jnp.where(lane == g_sel, 1.0, 0.0)
        r = lax.broadcasted_iota(I32, (tt, tt), 0)
        c = lax.broadcasted_iota(I32, (tt, tt), 1)
        before = jnp.where(r > c, 1.0, 0.0).astype(BF16)
        rank = jnp.dot(before, member.astype(BF16), preferred_element_type=F32)
        total = jnp.sum(member, axis=0, keepdims=True)
        lane1 = lax.broadcasted_iota(I32, (1, LANES), 1)
        start = sum(jnp.where(lane1 > k, total[:, k:k + 1], 0.0) for k in range(N_GROUPS))
        for k in range(N_GROUPS + 1):
            seg_ref[k] = start[0, k].astype(I32)
        pos = jnp.sum(member * (rank + start), axis=-1, keepdims=True)
        pos_row = jnp.broadcast_to(pos, (tt, LANES)).T[0:1].astype(I32)
        gather = jnp.where(r == pos_row, 1.0, 0.0).astype(BF16)
        pos_ref[...] = pos.astype(I32)
        hn_ref[...] = jnp.dot(gather, hn, preferred_element_type=F32).astype(BF16)
        hi = comb.astype(BF16)
        rest = comb - hi.astype(F32)
        mid = rest.astype(BF16)
        low = (rest - mid.astype(F32)).astype(BF16)
        comb_ref[...] = sum(jnp.dot(gather, part, preferred_element_type=F32) for part in (hi, mid, low))
        o_ref[...] = jnp.zeros_like(o_ref)

    lane_w = lax.broadcasted_iota(I32, (MOE_WINDOW, LANES), 1)
    for w in range(tt // MOE_WINDOW):
        rows = slice(w * MOE_WINDOW, (w + 1) * MOE_WINDOW)

        @pl.when((seg_ref[g] < (w + 1) * MOE_WINDOW) & (seg_ref[g + 1] > w * MOE_WINDOW))
        def _(rows=rows):
            hn = hn_ref[rows, :]
            comb = comb_ref[rows, :]
            y = o_ref[rows, :]
            for e in range(EXPERTS_PER_GROUP):
                col = N_GROUPS + g * EXPERTS_PER_GROUP + e
                cw = jnp.sum(jnp.where(lane_w == col, comb, 0.0), axis=-1, keepdims=True)
                gate = jnp.dot(hn, wg_ref[0, e], preferred_element_type=F32)
                up = jnp.dot(hn, wu_ref[0, e], preferred_element_type=F32)
                act = (gate * jax.nn.sigmoid(gate)) * up * cw
                y = y + jnp.dot(act.astype(BF16), wd_ref[0, e], preferred_element_type=F32)
            o_ref[rows, :] = y

    @pl.when(g == N_GROUPS - 1)
    def _():
        ys = o_ref[...]
        hi = ys.astype(BF16)
        low = (ys - hi.astype(F32)).astype(BF16)
        slot = lax.broadcasted_iota(I32, (tt, tt), 1)
        scatter = jnp.where(slot == pos_ref[...], 1.0, 0.0).astype(BF16)
        y = (jnp.dot(scatter, hi, preferred_element_type=F32)
             + jnp.dot(scatter, low, preferred_element_type=F32))
        o_ref[...] = _rms(x_ref[...] + y, gf_ref[...])


def _moe(x2d, gm, wr, br, wg, wu, wd, gf):
    T, D = x2d.shape
    tt = TT_MOE
    full = lambda a: pl.BlockSpec(a.shape, lambda i, g: (0,) * a.ndim)
    return pl.pallas_call(
        _moe_kernel,
        grid=(T // tt, N_GROUPS),
        in_specs=[pl.BlockSpec((tt, D), lambda i, g: (i, 0)), full(gm), full(wr), full(br),
                  pl.BlockSpec((1, EXPERTS_PER_GROUP, D, EXPERT_FF), lambda i, g: (g, 0, 0, 0)),
                  pl.BlockSpec((1, EXPERTS_PER_GROUP, D, EXPERT_FF), lambda i, g: (g, 0, 0, 0)),
                  pl.BlockSpec((1, EXPERTS_PER_GROUP, EXPERT_FF, D), lambda i, g: (g, 0, 0, 0)),
                  full(gf)],
        out_specs=pl.BlockSpec((tt, D), lambda i, g: (i, 0)),
        out_shape=jax.ShapeDtypeStruct((T, D), F32),
        scratch_shapes=[pltpu.VMEM((tt, D), BF16), pltpu.VMEM((tt, LANES), F32),
                        pltpu.VMEM((tt, 1), I32),
                        pltpu.SMEM((N_GROUPS + 1,), I32)],
        compiler_params=_cparams(("parallel", "arbitrary")),
        name="moe",
    )(x2d, gm, wr, br, wg, wu, wd, gf)


def kernel(x, mem, positions, norm_mix_g, w_in, w_o_attn, conv_w, conv_b, conv_ln_g, conv_ln_b,
           w_conv_out, w_out, norm_x_g, norm_mem_g, w_q_x, w_kv_x, w_o_x, norm_moe_g,
           w_router_group, b_router_group, w_router_expert, b_router_expert,
           w_exp_gate, w_exp_up, w_exp_down, norm_final_g):
    B, S, D = x.shape
    T = B * S
    depth = norm_mix_g.shape[0]

    inv_freq = ROPE_THETA ** (-jnp.arange(0, HEAD_DIM, 2, dtype=F32) / HEAD_DIM)
    ang = positions.astype(F32)[..., None] * inv_freq
    cos, sin = jnp.cos(ang), jnp.sin(ang)
    cos128 = jnp.tile(cos, (1, 1, 4)).reshape(T, LANES)
    sin128 = jnp.tile(jnp.concatenate([-sin, sin], axis=-1), (1, 1, 2)).reshape(T, LANES)

    row = lambda a: a.reshape(1, -1)
    for l in range(depth):
        w = w_in[l]
        wm = jnp.concatenate([w[:, 0:512], w[:, 512:1024], w[:, 1536:2048],
                              w[:, 2120:3144], w[:, 3144:5192]], axis=1).astype(BF16)
        ws = jnp.pad(w[:, 2048:2112], ((0, 0), (0, LANES - IDX_DIM))).astype(BF16)
        wvt = w[:, 1024:1536].T.astype(BF16)
        wwt = w[:, 2112:2120].T.astype(BF16)
        q, k, qi, vt_blk, ki, wi_t, u, gs = _in_proj(x.reshape(T, D), row(norm_mix_g[l]), wm, ws, wvt, wwt,
                                                    cos128, sin128)
        attn = _dsa(q, qi, wi_t, ki, k, vt_blk, B, S).reshape(B, S, ATTN_WIDTH)

        uc = _conv(u.reshape(B, S, CONV_CH), jnp.pad(conv_w[l], ((0, 1), (0, 0))), row(conv_b[l]),
                   row(conv_ln_g[l]), row(conv_ln_b[l]))
        kv = _mem_kv(mem, row(norm_mem_g[l]), w_kv_x[l].astype(BF16))
        x = _mix_xattn(x, attn, uc, gs.reshape(B, S, 2 * D), kv,
                       w_o_attn[l].astype(BF16), w_conv_out[l].astype(BF16), w_out[l].astype(BF16),
                       row(norm_x_g[l]), w_q_x[l].astype(BF16), w_o_x[l].astype(BF16))

        wr = jnp.concatenate([w_router_group[l],
                              w_router_expert[l].transpose(1, 0, 2).reshape(D, N_EXPERTS)], axis=1)
        wr = jnp.pad(wr, ((0, 0), (0, LANES - N_GROUPS - N_EXPERTS))).astype(BF16)
        br = jnp.pad(jnp.concatenate([b_router_group[l], b_router_expert[l].reshape(-1)]),
                     (0, LANES - N_GROUPS - N_EXPERTS)).reshape(1, LANES)
        last = l == depth - 1
        gf = row(norm_final_g) if last else None
        assert last, "the final norm is fused into the last layer's MoE kernel"
        x = _moe(x.reshape(T, D), row(norm_moe_g[l]), wr, br, w_exp_gate[l].astype(BF16),
                 w_exp_up[l].astype(BF16), w_exp_down[l].astype(BF16), gf).reshape(B, S, D)
    return x
```

```python
import functools

import jax
import jax.numpy as jnp
from jax import lax
from jax.experimental import pallas as pl
from jax.experimental.pallas import tpu as pltpu

F32 = jnp.float32
BF16 = jnp.bfloat16
I32 = jnp.int32

D_MODEL = 1024
CHUNK = 64
ROPE_THETA = 10000.0
EPS = 1e-6
ATTN_HEADS = 8
HEAD_DIM = 64
ATTN_WIDTH = ATTN_HEADS * HEAD_DIM
IDX_HEADS = 8
IDX_DIM = 64
MAX_TOPK = 256
CONV_CH = D_MODEL // 2
CONV_WIDTH = 31
X_HEADS = 4
X_HEAD_DIM = D_MODEL // X_HEADS
N_GROUPS = 4
EXPERTS_PER_GROUP = 8
N_EXPERTS = N_GROUPS * EXPERTS_PER_GROUP
EXPERT_FF = D_MODEL // 4

LANES = 128
SUBLANES = 8
VMEM_LIMIT = 56 * 1024 * 1024
NEG_BIG = -1e30

TM_IN = 256
QT = 256
KB = 128
SEARCH_FIXED_STEPS = 14
ACC_ROWS = HEAD_DIM + 16
LOG2E = 1.4426950408889634
TC_CONV = 256
HALO = 32
TM_MIX = 512
TT_MOE = 1024
MOE_WINDOW = 320
MOE_ALIGN = 16


def _cparams(sem):
    return pltpu.CompilerParams(dimension_semantics=sem, vmem_limit_bytes=VMEM_LIMIT)


def _rms(x, g):
    return x * lax.rsqrt(jnp.mean(x * x, axis=-1, keepdims=True) + EPS) * g


def _inproj_kernel(x_ref, g_ref, wm_ref, ws_ref, wvt_ref, wwt_ref, cos_ref, sin_ref,
                   q_ref, k_ref, qi_ref, vt_ref, ki_ref, wit_ref, u_ref, gs_ref):
    tm = x_ref.shape[0]
    nt_dims = (((1,), (1,)), ((), ()))
    h = _rms(x_ref[...], g_ref[...]).astype(BF16)
    cos = cos_ref[...]
    sin = sin_ref[...]
    lane = lax.broadcasted_iota(I32, (tm, LANES), 1)
    first_half = (lane % HEAD_DIM) < (HEAD_DIM // 2)

    def rope(y):
        rot = jnp.where(first_half, pltpu.roll(y, LANES - HEAD_DIM // 2, 1),
                        pltpu.roll(y, HEAD_DIM // 2, 1))
        return y * cos + rot * sin

    def proj(c0, w):
        return jnp.dot(h, wm_ref[:, c0:c0 + w], preferred_element_type=F32)

    for ref, c0, scale in ((q_ref, 0, HEAD_DIM ** -0.5 * LOG2E), (k_ref, 512, None), (qi_ref, 1024, None)):
        y = proj(c0, ATTN_WIDTH)
        for c in range(ATTN_WIDTH // LANES):
            r = rope(y[:, c * LANES:(c + 1) * LANES])
            if scale is not None:
                r = r * scale
            ref[:, c * LANES:(c + 1) * LANES] = r.astype(BF16)
    v_t = lax.dot_general(wvt_ref[...], h, nt_dims, preferred_element_type=F32)
    for c in range(tm // KB):
        vt_ref[c] = v_t[:, c * KB:(c + 1) * KB].astype(BF16)
    wit_ref[...] = lax.dot_general(wwt_ref[...], h, nt_dims, preferred_element_type=F32)
    glu = proj(1536, 2 * CONV_CH)
    u_ref[...] = glu[:, :CONV_CH] * jax.nn.sigmoid(glu[:, CONV_CH:])
    for c in range(4):
        gs_ref[:, c * 512:(c + 1) * 512] = jax.nn.sigmoid(proj(2560 + c * 512, 512))
    ys = jnp.dot(h, ws_ref[...], preferred_element_type=F32)
    ki_ref[...] = rope(ys)[:, :IDX_DIM].astype(BF16)


def _in_proj(x2d, g, wm, ws, wvt, wwt, cos128, sin128):
    T = x2d.shape[0]
    tm = TM_IN
    row = lambda w: pl.BlockSpec((tm, w), lambda i: (i, 0))
    full = lambda a: pl.BlockSpec(a.shape, lambda i: (0,) * a.ndim)
    return pl.pallas_call(
        _inproj_kernel,
        grid=(T // tm,),
        in_specs=[row(D_MODEL), full(g), full(wm), full(ws), full(wvt), full(wwt), row(LANES), row(LANES)],
        out_specs=[row(512), row(512), row(512),
                   pl.BlockSpec((tm // KB, ATTN_WIDTH, KB), lambda i: (i, 0, 0)),
                   row(IDX_DIM), pl.BlockSpec((IDX_HEADS, tm), lambda i: (0, i)),
                   row(CONV_CH), row(2 * D_MODEL)],
        out_shape=[jax.ShapeDtypeStruct((T, 512), BF16)] * 3
        + [jax.ShapeDtypeStruct((T // KB, ATTN_WIDTH, KB), BF16),
           jax.ShapeDtypeStruct((T, IDX_DIM), BF16), jax.ShapeDtypeStruct((IDX_HEADS, T), F32),
           jax.ShapeDtypeStruct((T, CONV_CH), F32), jax.ShapeDtypeStruct((T, 2 * D_MODEL), F32)],
        compiler_params=_cparams(("parallel",)),
        name="in_proj",
    )(x2d, g, wm, ws, wvt, wwt, cos128, sin128)


def _dsa_kernel(q_ref, qi_ref, wi_ref, ki_ref, k_ref, vt_ref, o_ref,
                sc_ref, bias_ref, acc_ref, sa_ref, sb_ref, pa_ref, pb_ref, qh_ref, qih_ref):
    assert QT == 2 * KB
    t = pl.program_id(1)
    n_qb = t + 1
    n_kb = 2 * n_qb
    scale_idx = (IDX_DIM ** -0.5) * (IDX_HEADS ** -0.5)
    nt_dims = (((1,), (1,)), ((), ()))
    top_k = float(MAX_TOPK)
    head = lambda h: slice(h * HEAD_DIM, (h + 1) * HEAD_DIM)

    qcol = lax.broadcasted_iota(I32, (1, QT), 1) + t * QT
    limit = (qcol // CHUNK + 1) * CHUNK
    w_idx = wi_ref[...]
    for h in range(ATTN_HEADS):
        qh_ref[h] = q_ref[:, head(h)]
        qih_ref[h] = qi_ref[:, head(h)]

    def score_rows(ks, masked):
        ki_blk = ki_ref[pl.ds(ks, KB), :]
        acc = jnp.zeros((KB, QT), F32)
        for h in range(IDX_HEADS):
            rel = lax.dot_general(ki_blk, qih_ref[h], nt_dims, preferred_element_type=F32)
            acc = acc + jnp.maximum(rel, 0.0) * w_idx[h:h + 1, :]
        score = acc * scale_idx
        if masked:
            adm = lax.broadcasted_iota(I32, (KB, QT), 0) + ks < limit
            lo_src = jnp.where(adm, score, jnp.inf)
            score = jnp.where(adm, score, -jnp.inf)
        else:
            lo_src = score
        sc_ref[pl.ds(ks, KB), :] = score
        fold = lambda a, op: functools.reduce(op, [a[r * 8:(r + 1) * 8] for r in range(KB // 8)])
        return fold(lo_src, jnp.minimum), fold(score, jnp.maximum)

    def score_block(qb, carry, masked=False):
        mn, mx = carry
        for sub in range(QT // KB):
            ks = pl.multiple_of(qb * QT + sub * KB, KB)
            bmn, bmx = score_rows(ks, masked)
            mn, mx = jnp.minimum(mn, bmn), jnp.maximum(mx, bmx)
        return mn, mx

    init = (jnp.full((8, QT), jnp.inf, F32), jnp.full((8, QT), -jnp.inf, F32))
    mn8, mx8 = score_block(t, lax.fori_loop(0, t, score_block, init), masked=True)
    s_min = jnp.min(mn8, axis=0, keepdims=True)
    s_max = jnp.max(mx8, axis=0, keepdims=True)

    n_pairs = (n_qb + 1) // 2

    @pl.when(n_qb % 2 == 1)
    def _():
        sc_ref[pl.ds(pl.multiple_of(n_qb * QT, QT), QT), :] = jnp.full((QT, QT), -jnp.inf, F32)

    def count(cand, strict=False):
        def body(pb, c):
            for sub in range(2):
                qs = pl.multiple_of((2 * pb + sub) * QT, QT)
                blk = sc_ref[pl.ds(qs, QT), :]
                hit = jnp.where(blk > cand if strict else blk >= cand, 1.0, 0.0)
                c = c + jnp.sum(hit.reshape(QT // 8, 8, QT), axis=0)
            return c
        c = lax.fori_loop(0, n_pairs, body, jnp.zeros((8, QT), F32))
        return jnp.sum(c, axis=0, keepdims=True)

    count_ge = count

    def search_step(c):
        lo, hi, c_lo, active = c
        mid = 0.5 * lo + 0.5 * hi
        cnt = count_ge(mid)
        up = (active > 0.0) & (cnt >= top_k)
        down = (active > 0.0) & (cnt < top_k)
        stuck = (mid <= lo) | (mid >= hi)
        lo, c_lo, hi = jnp.where(up, mid, lo), jnp.where(up, cnt, c_lo), jnp.where(down, mid, hi)
        active = jnp.where((c_lo > top_k) & jnp.logical_not(stuck), active, 0.0)
        return lo, hi, c_lo, active

    n_adm = limit.astype(F32)
    searching = n_adm > top_k
    zero = jnp.zeros((1, QT), F32)
    c_ge0, c_gt0 = count(zero), count(zero, strict=True)
    above = searching & (c_gt0 >= top_k)
    at_zero = searching & (c_gt0 < top_k) & (c_ge0 >= top_k)
    below = searching & (c_ge0 < top_k)
    state = (jnp.where(above | at_zero, zero, s_min), jnp.where(below | at_zero, zero, s_max),
             jnp.where(above | at_zero, c_ge0, n_adm),
             jnp.where(above | below, 1.0, 0.0))
    state = lax.fori_loop(0, SEARCH_FIXED_STEPS, lambda i, c: search_step(c), state)
    lo, hi, c_lo, _, _ = lax.while_loop(
        lambda c: jnp.logical_and(jnp.max(c[3]) > 0.0, c[4] < 200),
        lambda c: search_step(search_step(c[:4])) + (c[4] + 1,),
        state + (jnp.int32(0),))

    def bias_block(qb, carry):
        qs = pl.multiple_of(qb * QT, QT)
        bias_ref[pl.ds(qs, QT), :] = jnp.where(sc_ref[pl.ds(qs, QT), :] >= lo, 0.0, NEG_BIG)
        return carry

    lax.fori_loop(0, n_qb, bias_block, 0)

    @pl.when(jnp.max(c_lo) > top_k)
    def _():
        tau = jnp.where(count_ge(hi) >= top_k, hi, lo)
        need = top_k - count(tau, strict=True)
        r = lax.broadcasted_iota(I32, (QT, QT), 0)
        c = lax.broadcasted_iota(I32, (QT, QT), 1)
        lower = jnp.where(r > c, 1.0, 0.0).astype(BF16)

        def tie_block(qb, seen):
            qs = pl.multiple_of(qb * QT, QT)
            blk = sc_ref[pl.ds(qs, QT), :]
            eq = blk == tau
            eqf = jnp.where(eq, 1.0, 0.0)
            rank = jnp.dot(lower, eqf.astype(BF16), preferred_element_type=F32) + seen
            sel = (blk > tau) | (eq & (rank < need))
            bias_ref[pl.ds(qs, QT), :] = jnp.where(sel, 0.0, NEG_BIG)
            return seen + jnp.sum(eqf, axis=0, keepdims=True)

        lax.fori_loop(0, n_qb, tie_block, jnp.zeros((1, QT), F32))

    def logits(kb, s_ref):
        ks = pl.multiple_of(kb * KB, KB)
        bias = bias_ref[pl.ds(ks, KB), :]
        for h in range(ATTN_HEADS):
            s_ref[h] = lax.dot_general(k_ref[pl.ds(ks, KB), head(h)], qh_ref[h], nt_dims,
                                       preferred_element_type=F32) + bias

    def softmax(s_ref, p_ref, m):
        ms, alphas = [], []
        for h in range(ATTN_HEADS):
            s = s_ref[h]
            m_old = m[h:h + 1]
            m_new = jnp.maximum(m_old, jnp.max(s, axis=0, keepdims=True))
            p_ref[h] = jnp.exp2(s - m_new).astype(BF16)
            ms.append(m_new)
            alphas.append(jnp.exp2(m_old - m_new))
        cat = lambda rows: jnp.concatenate(rows, axis=0)
        return cat(ms), cat(alphas)

    ones_rows = jnp.ones((ACC_ROWS - HEAD_DIM, KB), BF16)

    def values(kb, p_ref, alpha):
        for h in range(ATTN_HEADS):
            lhs = jnp.concatenate([vt_ref[kb, head(h), :], ones_rows], axis=0)
            acc_ref[h] = acc_ref[h] * alpha[h:h + 1] + jnp.dot(lhs, p_ref[h],
                                                              preferred_element_type=F32)

    acc_ref[...] = jnp.zeros(acc_ref.shape, F32)
    pb_ref[...] = jnp.zeros(pb_ref.shape, BF16)
    logits(0, sa_ref)

    def attn_pair(j, carry):
        m, alpha_b = carry
        kb0 = 2 * j
        logits(kb0 + 1, sb_ref)
        m, alpha_a = softmax(sa_ref, pa_ref, m)
        values(jnp.maximum(kb0 - 1, 0), pb_ref, alpha_b)
        logits(jnp.minimum(kb0 + 2, n_kb - 1), sa_ref)
        m, alpha_b = softmax(sb_ref, pb_ref, m)
        values(kb0, pa_ref, alpha_a)
        return m, alpha_b

    heads_qt = (ATTN_HEADS, QT)
    _, alpha_b = lax.fori_loop(0, n_qb, attn_pair,
                               (jnp.full(heads_qt, NEG_BIG, F32), jnp.ones(heads_qt, F32)))
    values(n_kb - 1, pb_ref, alpha_b)
    for j in range(ATTN_HEADS // 2):
        pair = jnp.concatenate(
            [acc_ref[h, :HEAD_DIM, :] / acc_ref[h, HEAD_DIM:HEAD_DIM + 1, :] for h in (2 * j, 2 * j + 1)],
            axis=0)
        o_ref[:, j * LANES:(j + 1) * LANES] = pair.T.astype(BF16)


def _dsa(q, qi, wi_t, ki, k, vt_blk, B, S):
    nq = S // QT
    return pl.pallas_call(
        _dsa_kernel,
        grid=(B, nq),
        in_specs=[
            pl.BlockSpec((QT, ATTN_WIDTH), lambda b, t: (b * nq + t, 0)),
            pl.BlockSpec((QT, IDX_HEADS * IDX_DIM), lambda b, t: (b * nq + t, 0)),
            pl.BlockSpec((IDX_HEADS, QT), lambda b, t: (0, b * nq + t)),
            pl.BlockSpec((S, IDX_DIM), lambda b, t: (b, 0)),
            pl.BlockSpec((S, ATTN_WIDTH), lambda b, t: (b, 0)),
            pl.BlockSpec((S // KB, ATTN_WIDTH, KB), lambda b, t: (b, 0, 0)),
        ],
        out_specs=pl.BlockSpec((QT, ATTN_WIDTH), lambda b, t: (b * nq + t, 0)),
        out_shape=jax.ShapeDtypeStruct((B * S, ATTN_WIDTH), BF16),
        scratch_shapes=[pltpu.VMEM((S, QT), F32), pltpu.VMEM((S, QT), F32),
                        pltpu.VMEM((ATTN_HEADS, ACC_ROWS, QT), F32),
                        pltpu.VMEM((ATTN_HEADS, KB, QT), F32), pltpu.VMEM((ATTN_HEADS, KB, QT), F32),
                        pltpu.VMEM((ATTN_HEADS, KB, QT), BF16), pltpu.VMEM((ATTN_HEADS, KB, QT), BF16),
                        pltpu.VMEM((ATTN_HEADS, QT, HEAD_DIM), BF16),
                        pltpu.VMEM((IDX_HEADS, QT, IDX_DIM), BF16)],
        compiler_params=_cparams(("parallel", "arbitrary")),
        name="dsa",
    )(q, qi, wi_t, ki, k, vt_blk)


def _conv_kernel(u_ref, halo_ref, w_ref, b_ref, lg_ref, lb_ref, o_ref, buf_ref):
    i = pl.program_id(1)
    tc = u_ref.shape[1]
    halo = halo_ref[0]
    buf_ref[0, 0:HALO, :] = jnp.where(i > 0, halo, jnp.zeros_like(halo))
    buf_ref[0, HALO:HALO + tc, :] = u_ref[0]
    span = HALO + tc - SUBLANES
    for r in range(1, SUBLANES):
        buf_ref[r, 0:span, :] = buf_ref[0, r:r + span, :]
    off = HALO - (CONV_WIDTH - 1)
    rows = 64
    for rb in range(tc // rows):
        acc = jnp.zeros((rows, CONV_CH), F32) + b_ref[...]
        for j in range(CONV_WIDTH):
            r = (off + j) % SUBLANES
            s = rb * rows + off + j - r
            acc = acc + buf_ref[r, s:s + rows, :] * w_ref[j:j + 1, :]
        mu = jnp.mean(acc, axis=-1, keepdims=True)
        d = acc - mu
        var = jnp.mean(d * d, axis=-1, keepdims=True)
        y = d * lax.rsqrt(var + EPS) * lg_ref[...] + lb_ref[...]
        o_ref[0, rb * rows:(rb + 1) * rows, :] = (y * jax.nn.sigmoid(y)).astype(BF16)


def _conv(u, conv_w, conv_b, ln_g, ln_b):
    B, S, C = u.shape
    tc = TC_CONV
    full = lambda a: pl.BlockSpec(a.shape, lambda b, i: (0,) * a.ndim)
    return pl.pallas_call(
        _conv_kernel,
        grid=(B, S // tc),
        in_specs=[
            pl.BlockSpec((1, tc, C), lambda b, i: (b, i, 0)),
            pl.BlockSpec((1, HALO, C), lambda b, i: (b, jnp.maximum(i * (tc // HALO) - 1, 0), 0)),
            full(conv_w), full(conv_b), full(ln_g), full(ln_b),
        ],
        out_specs=pl.BlockSpec((1, tc, C), lambda b, i: (b, i, 0)),
        out_shape=jax.ShapeDtypeStruct((B, S, C), BF16),
        scratch_shapes=[pltpu.VMEM((SUBLANES, HALO + tc, C), F32)],
        compiler_params=_cparams(("parallel", "parallel")),
        name="conv",
    )(u, u, conv_w, conv_b, ln_g, ln_b)


def _memkv_kernel(mem_ref, g_ref, w_ref, o_ref):
    h = _rms(mem_ref[0], g_ref[...]).astype(BF16)
    o_ref[0] = jnp.dot(h, w_ref[...], preferred_element_type=F32).astype(BF16)


def _mem_kv(mem, g, w_kv):
    B, M, D = mem.shape
    return pl.pallas_call(
        _memkv_kernel,
        grid=(B,),
        in_specs=[pl.BlockSpec((1, M, D), lambda b: (b, 0, 0)),
                  pl.BlockSpec(g.shape, lambda b: (0, 0)),
                  pl.BlockSpec(w_kv.shape, lambda b: (0, 0))],
        out_specs=pl.BlockSpec((1, M, 2 * D), lambda b: (b, 0, 0)),
        out_shape=jax.ShapeDtypeStruct((B, M, 2 * D), BF16),
        compiler_params=_cparams(("parallel",)),
        name="mem_kv",
    )(mem, g, w_kv)


def _mix_kernel(x_ref, a_ref, c_ref, gs_ref, kv_ref, woa_ref, wco_ref, wout_ref, gx_ref,
                wq_ref, wox_ref, o_ref):
    dot = functools.partial(jnp.dot, preferred_element_type=F32)
    y_attn = dot(a_ref[0], woa_ref[...])
    y_conv = dot(c_ref[0], wco_ref[...])
    gs = gs_ref[0]
    merged = gs[:, :D_MODEL] * y_attn + gs[:, D_MODEL:] * y_conv
    x1 = x_ref[0] + dot(merged.astype(BF16), wout_ref[...])

    q = (dot(_rms(x1, gx_ref[...]).astype(BF16), wq_ref[...]) * (X_HEAD_DIM ** -0.5)).astype(BF16)
    heads = []
    for h in range(X_HEADS):
        lo = h * X_HEAD_DIM
        kh = kv_ref[0, :, lo:lo + X_HEAD_DIM]
        vh = kv_ref[0, :, D_MODEL + lo:D_MODEL + lo + X_HEAD_DIM]
        s = lax.dot_general(q[:, lo:lo + X_HEAD_DIM], kh, (((1,), (1,)), ((), ())),
                            preferred_element_type=F32)
        p = jnp.exp(s - jnp.max(s, axis=-1, keepdims=True))
        p = p / jnp.sum(p, axis=-1, keepdims=True)
        heads.append(dot(p.astype(BF16), vh))
    o = jnp.concatenate(heads, axis=-1)
    o_ref[0] = x1 + dot(o.astype(BF16), wox_ref[...])


def _mix_xattn(x, attn, uc, gs, kv, woa, wco, wout, gx, wq, wox):
    B, S, D = x.shape
    tm = TM_MIX
    row = lambda w: pl.BlockSpec((1, tm, w), lambda b, i: (b, i, 0))
    full = lambda a: pl.BlockSpec(a.shape, lambda b, i: (0,) * a.ndim)
    return pl.pallas_call(
        _mix_kernel,
        grid=(B, S // tm),
        in_specs=[row(D), row(ATTN_WIDTH), row(CONV_CH), row(2 * D),
                  pl.BlockSpec((1,) + kv.shape[1:], lambda b, i: (b, 0, 0)),
                  full(woa), full(wco), full(wout), full(gx), full(wq), full(wox)],
        out_specs=row(D),
        out_shape=jax.ShapeDtypeStruct((B, S, D), F32),
        compiler_params=_cparams(("parallel", "parallel")),
        name="mix_xattn",
    )(x, attn, uc, gs, kv, woa, wco, wout, gx, wq, wox)


def _moe_kernel(x_ref, gm_ref, wr_ref, br_ref, wg_ref, wu_ref, wd_ref, gf_ref, o_ref,
                hn_ref, comb_ref, pos_ref, seg_ref):
    g = pl.program_id(1)
    tt = x_ref.shape[0]
    lane = lax.broadcasted_iota(I32, (tt, LANES), 1)

    @pl.when(g == 0)
    def _():
        hn = _rms(x_ref[...], gm_ref[...]).astype(BF16)
        logits = jnp.dot(hn, wr_ref[...], preferred_element_type=F32) + br_ref[...]
        gl = jnp.where(lane < N_GROUPS, logits, -jnp.inf)
        gmax = jnp.max(gl, axis=-1, keepdims=True)
        g_sel = jnp.min(jnp.where(gl == gmax, lane, LANES), axis=-1, keepdims=True)
        g_gate = 1.0 / jnp.sum(jnp.exp(gl - gmax), axis=-1, keepdims=True)
        e_lo = N_GROUPS + g_sel * EXPERTS_PER_GROUP
        el = jnp.where((lane >= e_lo) & (lane < e_lo + EXPERTS_PER_GROUP), logits, -jnp.inf)
        top1 = jnp.max(el, axis=-1, keepdims=True)
        i1 = jnp.min(jnp.where(el == top1, lane, LANES), axis=-1, keepdims=True)
        el2 = jnp.where(lane == i1, -jnp.inf, el)
        top2 = jnp.max(el2, axis=-1, keepdims=True)
        i2 = jnp.min(jnp.where(el2 == top2, lane, LANES), axis=-1, keepdims=True)
        e2 = jnp.exp(top2 - top1)
        w1 = g_gate / (1.0 + e2)
        w2 = g_gate * e2 / (1.0 + e2)
        comb = jnp.where(lane == i1, w1, 0.0) + jnp.where(lane == i2, w2, 0.0)

        member = jnp.where(lane == g_sel, 1.0, 0.0)
        r = lax.broadcasted_iota(I32, (tt, tt), 0)
        c = lax.broadcasted_iota(I32, (tt, tt), 1)
        before = jnp.where(r > c, 1.0, 0.0).astype(BF16)
        rank = jnp.dot(before, member.astype(BF16), preferred_element_type=F32)
        total = jnp.sum(member, axis=0, keepdims=True)
        lane1 = lax.broadcasted_iota(I32, (1, LANES), 1)
        start = sum(jnp.where(lane1 > k, total[:, k:k + 1], 0.0) for k in range(N_GROUPS))
        for k in range(N_GROUPS + 1):
            seg_ref[k] = start[0, k].astype(I32)
        pos = jnp.sum(member * (rank + start), axis=-1, keepdims=True)
        pos_row = jnp.broadcast_to(pos, (tt, LANES)).T[0:1].astype(I32)
        gather = jnp.where(r == pos_row, 1.0, 0.0).astype(BF16)
        pos_ref[...] = pos.astype(I32)
        hn_ref[...] = jnp.dot(gather, hn, preferred_element_type=F32).astype(BF16)
        hi = comb.astype(BF16)
        rest = comb - hi.astype(F32)
        mid = rest.astype(BF16)
        low = (rest - mid.astype(F32)).astype(BF16)
        comb_ref[...] = sum(jnp.dot(gather, part, preferred_element_type=F32) for part in (hi, mid, low))
        o_ref[...] = jnp.zeros_like(o_ref)

    lane_w = lax.broadcasted_iota(I32, (MOE_WINDOW, LANES), 1)
    row_w = lax.broadcasted_iota(I32, (MOE_WINDOW, 1), 0)
    first = (seg_ref[g] // MOE_ALIGN) * MOE_ALIGN
    n_win = (seg_ref[g + 1] - first + MOE_WINDOW - 1) // MOE_WINDOW

    def window(k, carry):
        want = first + k * MOE_WINDOW
        st = pl.multiple_of(jnp.minimum(want, tt - MOE_WINDOW), MOE_ALIGN)
        rows = pl.ds(st, MOE_WINDOW)
        hn = hn_ref[rows, :]
        comb = jnp.where(row_w + st >= want, comb_ref[rows, :], 0.0)
        y = o_ref[rows, :]
        for e in range(EXPERTS_PER_GROUP):
            col = N_GROUPS + g * EXPERTS_PER_GROUP + e
            cw = jnp.sum(jnp.where(lane_w == col, comb, 0.0), axis=-1, keepdims=True)
            gate = jnp.dot(hn, wg_ref[0, e], preferred_element_type=F32)
            up = jnp.dot(hn, wu_ref[0, e], preferred_element_type=F32)
            act = (gate * jax.nn.sigmoid(gate)) * up * cw
            y = y + jnp.dot(act.astype(BF16), wd_ref[0, e], preferred_element_type=F32)
        o_ref[rows, :] = y
        return carry

    lax.fori_loop(0, n_win, window, 0)

    @pl.when(g == N_GROUPS - 1)
    def _():
        ys = o_ref[...]
        hi = ys.astype(BF16)
        low = (ys - hi.astype(F32)).astype(BF16)
        slot = lax.broadcasted_iota(I32, (tt, tt), 1)
        scatter = jnp.where(slot == pos_ref[...], 1.0, 0.0).astype(BF16)
        y = (jnp.dot(scatter, hi, preferred_element_type=F32)
             + jnp.dot(scatter, low, preferred_element_type=F32))
        o_ref[...] = _rms(x_ref[...] + y, gf_ref[...])


def _moe(x2d, gm, wr, br, wg, wu, wd, gf):
    T, D = x2d.shape
    tt = TT_MOE
    full = lambda a: pl.BlockSpec(a.shape, lambda i, g: (0,) * a.ndim)
    return pl.pallas_call(
        _moe_kernel,
        grid=(T // tt, N_GROUPS),
        in_specs=[pl.BlockSpec((tt, D), lambda i, g: (i, 0)), full(gm), full(wr), full(br),
                  pl.BlockSpec((1, EXPERTS_PER_GROUP, D, EXPERT_FF), lambda i, g: (g, 0, 0, 0)),
                  pl.BlockSpec((1, EXPERTS_PER_GROUP, D, EXPERT_FF), lambda i, g: (g, 0, 0, 0)),
                  pl.BlockSpec((1, EXPERTS_PER_GROUP, EXPERT_FF, D), lambda i, g: (g, 0, 0, 0)),
                  full(gf)],
        out_specs=pl.BlockSpec((tt, D), lambda i, g: (i, 0)),
        out_shape=jax.ShapeDtypeStruct((T, D), F32),
        scratch_shapes=[pltpu.VMEM((tt, D), BF16), pltpu.VMEM((tt, LANES), F32),
                        pltpu.VMEM((tt, 1), I32),
                        pltpu.SMEM((N_GROUPS + 1,), I32)],
        compiler_params=_cparams(("parallel", "arbitrary")),
        name="moe",
    )(x2d, gm, wr, br, wg, wu, wd, gf)


def kernel(x, mem, positions, norm_mix_g, w_in, w_o_attn, conv_w, conv_b, conv_ln_g, conv_ln_b,
           w_conv_out, w_out, norm_x_g, norm_mem_g, w_q_x, w_kv_x, w_o_x, norm_moe_g,
           w_router_group, b_router_group, w_router_expert, b_router_expert,
           w_exp_gate, w_exp_up, w_exp_down, norm_final_g):
    B, S, D = x.shape
    T = B * S
    depth = norm_mix_g.shape[0]

    inv_freq = ROPE_THETA ** (-jnp.arange(0, HEAD_DIM, 2, dtype=F32) / HEAD_DIM)
    ang = positions.astype(F32)[..., None] * inv_freq
    cos, sin = jnp.cos(ang), jnp.sin(ang)
    cos128 = jnp.tile(cos, (1, 1, 4)).reshape(T, LANES)
    sin128 = jnp.tile(jnp.concatenate([-sin, sin], axis=-1), (1, 1, 2)).reshape(T, LANES)

    row = lambda a: a.reshape(1, -1)
    for l in range(depth):
        w = w_in[l]
        wm = jnp.concatenate([w[:, 0:512], w[:, 512:1024], w[:, 1536:2048],
                              w[:, 2120:3144], w[:, 3144:5192]], axis=1).astype(BF16)
        ws = jnp.pad(w[:, 2048:2112], ((0, 0), (0, LANES - IDX_DIM))).astype(BF16)
        wvt = w[:, 1024:1536].T.astype(BF16)
        wwt = w[:, 2112:2120].T.astype(BF16)
        q, k, qi, vt_blk, ki, wi_t, u, gs = _in_proj(x.reshape(T, D), row(norm_mix_g[l]), wm, ws, wvt, wwt,
                                                    cos128, sin128)
        attn = _dsa(q, qi, wi_t, ki, k, vt_blk, B, S).reshape(B, S, ATTN_WIDTH)

        uc = _conv(u.reshape(B, S, CONV_CH), jnp.pad(conv_w[l], ((0, 1), (0, 0))), row(conv_b[l]),
                   row(conv_ln_g[l]), row(conv_ln_b[l]))
        kv = _mem_kv(mem, row(norm_mem_g[l]), w_kv_x[l].astype(BF16))
        x = _mix_xattn(x, attn, uc, gs.reshape(B, S, 2 * D), kv,
                       w_o_attn[l].astype(BF16), w_conv_out[l].astype(BF16), w_out[l].astype(BF16),
                       row(norm_x_g[l]), w_q_x[l].astype(BF16), w_o_x[l].astype(BF16))

        wr = jnp.concatenate([w_router_group[l],
                              w_router_expert[l].transpose(1, 0, 2).reshape(D, N_EXPERTS)], axis=1)
        wr = jnp.pad(wr, ((0, 0), (0, LANES - N_GROUPS - N_EXPERTS))).astype(BF16)
        br = jnp.pad(jnp.concatenate([b_router_group[l], b_router_expert[l].reshape(-1)]),
                     (0, LANES - N_GROUPS - N_EXPERTS)).reshape(1, LANES)
        last = l == depth - 1
        gf = row(norm_final_g) if last else None
        assert last, "the final norm is fused into the last layer's MoE kernel"
        x = _moe(x.reshape(T, D), row(norm_moe_g[l]), wr, br, w_exp_gate[l].astype(BF16),
                 w_exp_up[l].astype(BF16), w_exp_down[l].astype(BF16), gf).reshape(B, S, D)
    return x
```

```python
import functools

import jax
import jax.numpy as jnp
from jax import lax
from jax.experimental import pallas as pl
from jax.experimental.pallas import tpu as pltpu

F32 = jnp.float32
BF16 = jnp.bfloat16
I32 = jnp.int32

D_MODEL = 1024
CHUNK = 64
ROPE_THETA = 10000.0
EPS = 1e-6
ATTN_HEADS = 8
HEAD_DIM = 64
ATTN_WIDTH = ATTN_HEADS * HEAD_DIM
IDX_HEADS = 8
IDX_DIM = 64
MAX_TOPK = 256
CONV_CH = D_MODEL // 2
CONV_WIDTH = 31
X_HEADS = 4
X_HEAD_DIM = D_MODEL // X_HEADS
N_GROUPS = 4
EXPERTS_PER_GROUP = 8
N_EXPERTS = N_GROUPS * EXPERTS_PER_GROUP
EXPERT_FF = D_MODEL // 4

LANES = 128
SUBLANES = 8
VMEM_LIMIT = 56 * 1024 * 1024
NEG_BIG = -1e30

TM_IN = 256
QT = 256
KB = 128
SEARCH_FIXED_STEPS = 14
ACC_ROWS = HEAD_DIM + 16
LOG2E = 1.4426950408889634
TC_CONV = 256
HALO = 32
TM_MIX = 512
TT_MOE = 1024
MOE_WINDOW = 320
MOE_ALIGN = 16


def _cparams(sem):
    return pltpu.CompilerParams(dimension_semantics=sem, vmem_limit_bytes=VMEM_LIMIT)


def _rms(x, g):
    return x * lax.rsqrt(jnp.mean(x * x, axis=-1, keepdims=True) + EPS) * g


def _inproj_kernel(x_ref, g_ref, wm_ref, ws_ref, wvt_ref, wwt_ref, cos_ref, sin_ref,
                   cw_ref, cb_ref, lg_ref, lb_ref,
                   q_ref, k_ref, qi_ref, vt_ref, ki_ref, wit_ref, u_ref, gs_ref,
                   buf_ref, *, tiles_per_seq):
    tm = x_ref.shape[0]
    nt_dims = (((1,), (1,)), ((), ()))
    h = _rms(x_ref[...], g_ref[...]).astype(BF16)
    cos = cos_ref[...]
    sin = sin_ref[...]
    lane = lax.broadcasted_iota(I32, (tm, LANES), 1)
    first_half = (lane % HEAD_DIM) < (HEAD_DIM // 2)

    def rope(y):
        rot = jnp.where(first_half, pltpu.roll(y, LANES - HEAD_DIM // 2, 1),
                        pltpu.roll(y, HEAD_DIM // 2, 1))
        return y * cos + rot * sin

    def proj(c0, w):
        return jnp.dot(h, wm_ref[:, c0:c0 + w], preferred_element_type=F32)

    glu = proj(1536, 2 * CONV_CH)
    u = glu[:, :CONV_CH] * jax.nn.sigmoid(glu[:, CONV_CH:])

    i = pl.program_id(0)

    @pl.when(i == 0)
    def _():
        buf_ref[0] = jnp.zeros(buf_ref.shape[1:], F32)

    span = HALO + tm - SUBLANES
    for r in range(1, SUBLANES):
        buf_ref[r, 0:span, :] = buf_ref[0, r:r + span, :]
    off = HALO - (CONV_WIDTH - 1)
    rows = 64
    for rb in range(tm // rows):
        acc = jnp.zeros((rows, CONV_CH), F32) + cb_ref[...]
        for j in range(CONV_WIDTH):
            r = (off + j) % SUBLANES
            s = rb * rows + off + j - r
            acc = acc + buf_ref[r, s:s + rows, :] * cw_ref[j:j + 1, :]
        mu = jnp.mean(acc, axis=-1, keepdims=True)
        d = acc - mu
        var = jnp.mean(d * d, axis=-1, keepdims=True)
        y = d * lax.rsqrt(var + EPS) * lg_ref[...] + lb_ref[...]
        u_ref[rb * rows:(rb + 1) * rows, :] = (y * jax.nn.sigmoid(y)).astype(BF16)
    tail = buf_ref[0, tm:tm + HALO, :]
    buf_ref[0, 0:HALO, :] = jnp.where(i % tiles_per_seq > 0, tail, jnp.zeros_like(tail))
    buf_ref[0, HALO:HALO + tm, :] = u

    for ref, c0, scale in ((q_ref, 0, HEAD_DIM ** -0.5 * LOG2E), (k_ref, 512, None), (qi_ref, 1024, None)):
        y = proj(c0, ATTN_WIDTH)
        for c in range(ATTN_WIDTH // LANES):
            r = rope(y[:, c * LANES:(c + 1) * LANES])
            if scale is not None:
                r = r * scale
            ref[:, c * LANES:(c + 1) * LANES] = r.astype(BF16)
    v_t = lax.dot_general(wvt_ref[...], h, nt_dims, preferred_element_type=F32)
    for c in range(tm // KB):
        vt_ref[c] = v_t[:, c * KB:(c + 1) * KB].astype(BF16)
    wit_ref[...] = lax.dot_general(wwt_ref[...], h, nt_dims, preferred_element_type=F32)
    for c in range(4):
        gs_ref[:, c * 512:(c + 1) * 512] = jax.nn.sigmoid(proj(2560 + c * 512, 512))
    ys = jnp.dot(h, ws_ref[...], preferred_element_type=F32)
    ki_ref[...] = rope(ys)[:, :IDX_DIM].astype(BF16)


def _in_proj(x2d, g, wm, ws, wvt, wwt, cos128, sin128, conv_w, conv_b, ln_g, ln_b, seq_len):
    T = x2d.shape[0]
    tm = TM_IN
    n = T // tm
    cur = lambda i: jnp.minimum(i, n - 1)
    row = lambda w: pl.BlockSpec((tm, w), lambda i: (cur(i), 0))
    full = lambda a: pl.BlockSpec(a.shape, lambda i: (0,) * a.ndim)
    return pl.pallas_call(
        functools.partial(_inproj_kernel, tiles_per_seq=seq_len // tm),
        grid=(n + 1,),
        in_specs=[row(D_MODEL), full(g), full(wm), full(ws), full(wvt), full(wwt), row(LANES), row(LANES),
                  full(conv_w), full(conv_b), full(ln_g), full(ln_b)],
        out_specs=[row(512), row(512), row(512),
                   pl.BlockSpec((tm // KB, ATTN_WIDTH, KB), lambda i: (cur(i), 0, 0)),
                   row(IDX_DIM), pl.BlockSpec((IDX_HEADS, tm), lambda i: (0, cur(i))),
                   pl.BlockSpec((tm, CONV_CH), lambda i: (jnp.maximum(i - 1, 0), 0)),
                   row(2 * D_MODEL)],
        out_shape=[jax.ShapeDtypeStruct((T, 512), BF16)] * 3
        + [jax.ShapeDtypeStruct((T // KB, ATTN_WIDTH, KB), BF16),
           jax.ShapeDtypeStruct((T, IDX_DIM), BF16), jax.ShapeDtypeStruct((IDX_HEADS, T), F32),
           jax.ShapeDtypeStruct((T, CONV_CH), BF16), jax.ShapeDtypeStruct((T, 2 * D_MODEL), F32)],
        scratch_shapes=[pltpu.VMEM((SUBLANES, HALO + tm, CONV_CH), F32)],
        compiler_params=_cparams(("arbitrary",)),
        name="in_proj",
    )(x2d, g, wm, ws, wvt, wwt, cos128, sin128, conv_w, conv_b, ln_g, ln_b)


def _dsa_kernel(q_ref, qi_ref, wi_ref, ki_ref, k_ref, vt_ref, o_ref,
                sc_ref, bias_ref, acc_ref, sa_ref, sb_ref, pa_ref, pb_ref, qh_ref, qih_ref):
    assert QT == 2 * KB
    t = pl.program_id(1)
    n_qb = t + 1
    n_kb = 2 * n_qb
    scale_idx = (IDX_DIM ** -0.5) * (IDX_HEADS ** -0.5)
    nt_dims = (((1,), (1,)), ((), ()))
    top_k = float(MAX_TOPK)
    head = lambda h: slice(h * HEAD_DIM, (h + 1) * HEAD_DIM)

    qcol = lax.broadcasted_iota(I32, (1, QT), 1) + t * QT
    limit = (qcol // CHUNK + 1) * CHUNK
    w_idx = wi_ref[...]
    for h in range(ATTN_HEADS):
        qh_ref[h] = q_ref[:, head(h)]
        qih_ref[h] = qi_ref[:, head(h)]

    def score_rows(ks, masked):
        ki_blk = ki_ref[pl.ds(ks, KB), :]
        acc = jnp.zeros((KB, QT), F32)
        for h in range(IDX_HEADS):
            rel = lax.dot_general(ki_blk, qih_ref[h], nt_dims, preferred_element_type=F32)
            acc = acc + jnp.maximum(rel, 0.0) * w_idx[h:h + 1, :]
        score = acc * scale_idx
        if masked:
            adm = lax.broadcasted_iota(I32, (KB, QT), 0) + ks < limit
            lo_src = jnp.where(adm, score, jnp.inf)
            score = jnp.where(adm, score, -jnp.inf)
        else:
            lo_src = score
        sc_ref[pl.ds(ks, KB), :] = score
        fold = lambda a, op: functools.reduce(op, [a[r * 8:(r + 1) * 8] for r in range(KB // 8)])
        return fold(lo_src, jnp.minimum), fold(score, jnp.maximum)

    def score_block(qb, carry, masked=False):
        mn, mx = carry
        for sub in range(QT // KB):
            ks = pl.multiple_of(qb * QT + sub * KB, KB)
            bmn, bmx = score_rows(ks, masked)
            mn, mx = jnp.minimum(mn, bmn), jnp.maximum(mx, bmx)
        return mn, mx

    init = (jnp.full((8, QT), jnp.inf, F32), jnp.full((8, QT), -jnp.inf, F32))
    mn8, mx8 = score_block(t, lax.fori_loop(0, t, score_block, init), masked=True)
    s_min = jnp.min(mn8, axis=0, keepdims=True)
    s_max = jnp.max(mx8, axis=0, keepdims=True)

    n_pairs = (n_qb + 1) // 2

    @pl.when(n_qb % 2 == 1)
    def _():
        sc_ref[pl.ds(pl.multiple_of(n_qb * QT, QT), QT), :] = jnp.full((QT, QT), -jnp.inf, F32)

    def count(cand, strict=False):
        def body(pb, c):
            for sub in range(2):
                qs = pl.multiple_of((2 * pb + sub) * QT, QT)
                blk = sc_ref[pl.ds(qs, QT), :]
                hit = jnp.where(blk > cand if strict else blk >= cand, 1.0, 0.0)
                c = c + jnp.sum(hit.reshape(QT // 8, 8, QT), axis=0)
            return c
        c = lax.fori_loop(0, n_pairs, body, jnp.zeros((8, QT), F32))
        return jnp.sum(c, axis=0, keepdims=True)

    count_ge = count

    def search_step(c):
        lo, hi, c_lo, active = c
        mid = 0.5 * lo + 0.5 * hi
        cnt = count_ge(mid)
        up = (active > 0.0) & (cnt >= top_k)
        down = (active > 0.0) & (cnt < top_k)
        stuck = (mid <= lo) | (mid >= hi)
        lo, c_lo, hi = jnp.where(up, mid, lo), jnp.where(up, cnt, c_lo), jnp.where(down, mid, hi)
        active = jnp.where((c_lo > top_k) & jnp.logical_not(stuck), active, 0.0)
        return lo, hi, c_lo, active

    n_adm = limit.astype(F32)
    searching = n_adm > top_k
    zero = jnp.zeros((1, QT), F32)
    c_ge0, c_gt0 = count(zero), count(zero, strict=True)
    above = searching & (c_gt0 >= top_k)
    at_zero = searching & (c_gt0 < top_k) & (c_ge0 >= top_k)
    below = searching & (c_ge0 < top_k)
    state = (jnp.where(above | at_zero, zero, s_min), jnp.where(below | at_zero, zero, s_max),
             jnp.where(above | at_zero, c_ge0, n_adm),
             jnp.where(above | below, 1.0, 0.0))
    state = lax.fori_loop(0, SEARCH_FIXED_STEPS, lambda i, c: search_step(c), state)
    lo, hi, c_lo, _, _ = lax.while_loop(
        lambda c: jnp.logical_and(jnp.max(c[3]) > 0.0, c[4] < 200),
        lambda c: search_step(search_step(c[:4])) + (c[4] + 1,),
        state + (jnp.int32(0),))

    def bias_block(qb, carry):
        qs = pl.multiple_of(qb * QT, QT)
        bias_ref[pl.ds(qs, QT), :] = jnp.where(sc_ref[pl.ds(qs, QT), :] >= lo, 0.0, NEG_BIG)
        return carry

    lax.fori_loop(0, n_qb, bias_block, 0)

    @pl.when(jnp.max(c_lo) > top_k)
    def _():
        tau = jnp.where(count_ge(hi) >= top_k, hi, lo)
        need = top_k - count(tau, strict=True)
        r = lax.broadcasted_iota(I32, (QT, QT), 0)
        c = lax.broadcasted_iota(I32, (QT, QT), 1)
        lower = jnp.where(r > c, 1.0, 0.0).astype(BF16)

        def tie_block(qb, seen):
            qs = pl.multiple_of(qb * QT, QT)
            blk = sc_ref[pl.ds(qs, QT), :]
            eq = blk == tau
            eqf = jnp.where(eq, 1.0, 0.0)
            rank = jnp.dot(lower, eqf.astype(BF16), preferred_element_type=F32) + seen
            sel = (blk > tau) | (eq & (rank < need))
            bias_ref[pl.ds(qs, QT), :] = jnp.where(sel, 0.0, NEG_BIG)
            return seen + jnp.sum(eqf, axis=0, keepdims=True)

        lax.fori_loop(0, n_qb, tie_block, jnp.zeros((1, QT), F32))

    def logits(kb, s_ref):
        ks = pl.multiple_of(kb * KB, KB)
        bias = bias_ref[pl.ds(ks, KB), :]
        for h in range(ATTN_HEADS):
            s_ref[h] = lax.dot_general(k_ref[pl.ds(ks, KB), head(h)], qh_ref[h], nt_dims,
                                       preferred_element_type=F32) + bias

    def softmax(s_ref, p_ref, m):
        ms, alphas = [], []
        for h in range(ATTN_HEADS):
            s = s_ref[h]
            m_old = m[h:h + 1]
            m_new = jnp.maximum(m_old, jnp.max(s, axis=0, keepdims=True))
            p_ref[h] = jnp.exp2(s - m_new).astype(BF16)
            ms.append(m_new)
            alphas.append(jnp.exp2(m_old - m_new))
        cat = lambda rows: jnp.concatenate(rows, axis=0)
        return cat(ms), cat(alphas)

    ones_rows = jnp.ones((ACC_ROWS - HEAD_DIM, KB), BF16)

    def values(kb, p_ref, alpha):
        for h in range(ATTN_HEADS):
            lhs = jnp.concatenate([vt_ref[kb, head(h), :], ones_rows], axis=0)
            acc_ref[h] = acc_ref[h] * alpha[h:h + 1] + jnp.dot(lhs, p_ref[h],
                                                              preferred_element_type=F32)

    acc_ref[...] = jnp.zeros(acc_ref.shape, F32)
    pb_ref[...] = jnp.zeros(pb_ref.shape, BF16)
    logits(0, sa_ref)

    def attn_pair(j, carry):
        m, alpha_b = carry
        kb0 = 2 * j
        logits(kb0 + 1, sb_ref)
        m, alpha_a = softmax(sa_ref, pa_ref, m)
        values(jnp.maximum(kb0 - 1, 0), pb_ref, alpha_b)
        logits(jnp.minimum(kb0 + 2, n_kb - 1), sa_ref)
        m, alpha_b = softmax(sb_ref, pb_ref, m)
        values(kb0, pa_ref, alpha_a)
        return m, alpha_b

    heads_qt = (ATTN_HEADS, QT)
    _, alpha_b = lax.fori_loop(0, n_qb, attn_pair,
                               (jnp.full(heads_qt, NEG_BIG, F32), jnp.ones(heads_qt, F32)))
    values(n_kb - 1, pb_ref, alpha_b)
    for j in range(ATTN_HEADS // 2):
        pair = jnp.concatenate(
            [acc_ref[h, :HEAD_DIM, :] / acc_ref[h, HEAD_DIM:HEAD_DIM + 1, :] for h in (2 * j, 2 * j + 1)],
            axis=0)
        o_ref[:, j * LANES:(j + 1) * LANES] = pair.T.astype(BF16)


def _dsa(q, qi, wi_t, ki, k, vt_blk, B, S):
    nq = S // QT
    return pl.pallas_call(
        _dsa_kernel,
        grid=(B, nq),
        in_specs=[
            pl.BlockSpec((QT, ATTN_WIDTH), lambda b, t: (b * nq + t, 0)),
            pl.BlockSpec((QT, IDX_HEADS * IDX_DIM), lambda b, t: (b * nq + t, 0)),
            pl.BlockSpec((IDX_HEADS, QT), lambda b, t: (0, b * nq + t)),
            pl.BlockSpec((S, IDX_DIM), lambda b, t: (b, 0)),
            pl.BlockSpec((S, ATTN_WIDTH), lambda b, t: (b, 0)),
            pl.BlockSpec((S // KB, ATTN_WIDTH, KB), lambda b, t: (b, 0, 0)),
        ],
        out_specs=pl.BlockSpec((QT, ATTN_WIDTH), lambda b, t: (b * nq + t, 0)),
        out_shape=jax.ShapeDtypeStruct((B * S, ATTN_WIDTH), BF16),
        scratch_shapes=[pltpu.VMEM((S, QT), F32), pltpu.VMEM((S, QT), F32),
                        pltpu.VMEM((ATTN_HEADS, ACC_ROWS, QT), F32),
                        pltpu.VMEM((ATTN_HEADS, KB, QT), F32), pltpu.VMEM((ATTN_HEADS, KB, QT), F32),
                        pltpu.VMEM((ATTN_HEADS, KB, QT), BF16), pltpu.VMEM((ATTN_HEADS, KB, QT), BF16),
                        pltpu.VMEM((ATTN_HEADS, QT, HEAD_DIM), BF16),
                        pltpu.VMEM((IDX_HEADS, QT, IDX_DIM), BF16)],
        compiler_params=_cparams(("parallel", "arbitrary")),
        name="dsa",
    )(q, qi, wi_t, ki, k, vt_blk)


def _memkv_kernel(mem_ref, g_ref, w_ref, o_ref):
    h = _rms(mem_ref[0], g_ref[...]).astype(BF16)
    o_ref[0] = jnp.dot(h, w_ref[...], preferred_element_type=F32).astype(BF16)


def _mem_kv(mem, g, w_kv):
    B, M, D = mem.shape
    return pl.pallas_call(
        _memkv_kernel,
        grid=(B,),
        in_specs=[pl.BlockSpec((1, M, D), lambda b: (b, 0, 0)),
                  pl.BlockSpec(g.shape, lambda b: (0, 0)),
                  pl.BlockSpec(w_kv.shape, lambda b: (0, 0))],
        out_specs=pl.BlockSpec((1, M, 2 * D), lambda b: (b, 0, 0)),
        out_shape=jax.ShapeDtypeStruct((B, M, 2 * D), BF16),
        compiler_params=_cparams(("parallel",)),
        name="mem_kv",
    )(mem, g, w_kv)


def _mix_kernel(x_ref, a_ref, c_ref, gs_ref, kv_ref, woa_ref, wco_ref, wout_ref, gx_ref,
                wq_ref, wox_ref, o_ref):
    dot = functools.partial(jnp.dot, preferred_element_type=F32)
    y_attn = dot(a_ref[0], woa_ref[...])
    y_conv = dot(c_ref[0], wco_ref[...])
    gs = gs_ref[0]
    merged = gs[:, :D_MODEL] * y_attn + gs[:, D_MODEL:] * y_conv
    x1 = x_ref[0] + dot(merged.astype(BF16), wout_ref[...])

    q = (dot(_rms(x1, gx_ref[...]).astype(BF16), wq_ref[...]) * (X_HEAD_DIM ** -0.5)).astype(BF16)
    heads = []
    for h in range(X_HEADS):
        lo = h * X_HEAD_DIM
        kh = kv_ref[0, :, lo:lo + X_HEAD_DIM]
        vh = kv_ref[0, :, D_MODEL + lo:D_MODEL + lo + X_HEAD_DIM]
        s = lax.dot_general(q[:, lo:lo + X_HEAD_DIM], kh, (((1,), (1,)), ((), ())),
                            preferred_element_type=F32)
        p = jnp.exp(s - jnp.max(s, axis=-1, keepdims=True))
        p = p / jnp.sum(p, axis=-1, keepdims=True)
        heads.append(dot(p.astype(BF16), vh))
    o = jnp.concatenate(heads, axis=-1)
    o_ref[0] = x1 + dot(o.astype(BF16), wox_ref[...])


def _mix_xattn(x, attn, uc, gs, kv, woa, wco, wout, gx, wq, wox):
    B, S, D = x.shape
    tm = TM_MIX
    row = lambda w: pl.BlockSpec((1, tm, w), lambda b, i: (b, i, 0))
    full = lambda a: pl.BlockSpec(a.shape, lambda b, i: (0,) * a.ndim)
    return pl.pallas_call(
        _mix_kernel,
        grid=(B, S // tm),
        in_specs=[row(D), row(ATTN_WIDTH), row(CONV_CH), row(2 * D),
                  pl.BlockSpec((1,) + kv.shape[1:], lambda b, i: (b, 0, 0)),
                  full(woa), full(wco), full(wout), full(gx), full(wq), full(wox)],
        out_specs=row(D),
        out_shape=jax.ShapeDtypeStruct((B, S, D), F32),
        compiler_params=_cparams(("parallel", "parallel")),
        name="mix_xattn",
    )(x, attn, uc, gs, kv, woa, wco, wout, gx, wq, wox)


def _moe_kernel(x_ref, gm_ref, wr_ref, br_ref, wg_ref, wu_ref, wd_ref, gf_ref, o_ref,
                hn_ref, comb_ref, pos_ref, seg_ref):
    g = pl.program_id(1)
    tt = x_ref.shape[0]
    lane = lax.broadcasted_iota(I32, (tt, LANES), 1)

    @pl.when(g == 0)
    def _():
        hn = _rms(x_ref[...], gm_ref[...]).astype(BF16)
        logits = jnp.dot(hn, wr_ref[...], preferred_element_type=F32) + br_ref[...]
        gl = jnp.where(lane < N_GROUPS, logits, -jnp.inf)
        gmax = jnp.max(gl, axis=-1, keepdims=True)
        g_sel = jnp.min(jnp.where(gl == gmax, lane, LANES), axis=-1, keepdims=True)
        g_gate = 1.0 / jnp.sum(jnp.exp(gl - gmax), axis=-1, keepdims=True)
        e_lo = N_GROUPS + g_sel * EXPERTS_PER_GROUP
        el = jnp.where((lane >= e_lo) & (lane < e_lo + EXPERTS_PER_GROUP), logits, -jnp.inf)
        top1 = jnp.max(el, axis=-1, keepdims=True)
        i1 = jnp.min(jnp.where(el == top1, lane, LANES), axis=-1, keepdims=True)
        el2 = jnp.where(lane == i1, -jnp.inf, el)
        top2 = jnp.max(el2, axis=-1, keepdims=True)
        i2 = jnp.min(jnp.where(el2 == top2, lane, LANES), axis=-1, keepdims=True)
        e2 = jnp.exp(top2 - top1)
        w1 = g_gate / (1.0 + e2)
        w2 = g_gate * e2 / (1.0 + e2)
        comb = jnp.where(lane == i1, w1, 0.0) + jnp.where(lane == i2, w2, 0.0)

        member = jnp.where(lane == g_sel, 1.0, 0.0)
        r = lax.broadcasted_iota(I32, (tt, tt), 0)
        c = lax.broadcasted_iota(I32, (tt, tt), 1)
        before = jnp.where(r > c, 1.0, 0.0).astype(BF16)
        rank = jnp.dot(before, member.astype(BF16), preferred_element_type=F32)
        total = jnp.sum(member, axis=0, keepdims=True)
        lane1 = lax.broadcasted_iota(I32, (1, LANES), 1)
        start = sum(jnp.where(lane1 > k, total[:, k:k + 1], 0.0) for k in range(N_GROUPS))
        for k in range(N_GROUPS + 1):
            seg_ref[k] = start[0, k].astype(I32)
        pos = jnp.sum(member * (rank + start), axis=-1, keepdims=True)
        pos_row = jnp.broadcast_to(pos, (tt, LANES)).T[0:1].astype(I32)
        gather = jnp.where(r == pos_row, 1.0, 0.0).astype(BF16)
        pos_ref[...] = pos.astype(I32)
        hn_ref[...] = jnp.dot(gather, hn, preferred_element_type=F32).astype(BF16)
        hi = comb.astype(BF16)
        rest = comb - hi.astype(F32)
        mid = rest.astype(BF16)
        low = (rest - mid.astype(F32)).astype(BF16)
        comb_ref[...] = sum(jnp.dot(gather, part, preferred_element_type=F32) for part in (hi, mid, low))
        o_ref[...] = jnp.zeros_like(o_ref)

    lane_w = lax.broadcasted_iota(I32, (MOE_WINDOW, LANES), 1)
    row_w = lax.broadcasted_iota(I32, (MOE_WINDOW, 1), 0)
    first = (seg_ref[g] // MOE_ALIGN) * MOE_ALIGN
    n_win = (seg_ref[g + 1] - first + MOE_WINDOW - 1) // MOE_WINDOW

    def window(k, carry):
        want = first + k * MOE_WINDOW
        st = pl.multiple_of(jnp.minimum(want, tt - MOE_WINDOW), MOE_ALIGN)
        rows = pl.ds(st, MOE_WINDOW)
        hn = hn_ref[rows, :]
        comb = jnp.where(row_w + st >= want, comb_ref[rows, :], 0.0)
        y = o_ref[rows, :]
        for e in range(EXPERTS_PER_GROUP):
            col = N_GROUPS + g * EXPERTS_PER_GROUP + e
            cw = jnp.sum(jnp.where(lane_w == col, comb, 0.0), axis=-1, keepdims=True)
            gate = jnp.dot(hn, wg_ref[0, e], preferred_element_type=F32)
            up = jnp.dot(hn, wu_ref[0, e], preferred_element_type=F32)
            act = (gate * jax.nn.sigmoid(gate)) * up * cw
            y = y + jnp.dot(act.astype(BF16), wd_ref[0, e], preferred_element_type=F32)
        o_ref[rows, :] = y
        return carry

    lax.fori_loop(0, n_win, window, 0)

    @pl.when(g == N_GROUPS - 1)
    def _():
        ys = o_ref[...]
        hi = ys.astype(BF16)
        low = (ys - hi.astype(F32)).astype(BF16)
        slot = lax.broadcasted_iota(I32, (tt, tt), 1)
        scatter = jnp.where(slot == pos_ref[...], 1.0, 0.0).astype(BF16)
        y = (jnp.dot(scatter, hi, preferred_element_type=F32)
             + jnp.dot(scatter, low, preferred_element_type=F32))
        o_ref[...] = _rms(x_ref[...] + y, gf_ref[...])


def _moe(x2d, gm, wr, br, wg, wu, wd, gf):
    T, D = x2d.shape
    tt = TT_MOE
    full = lambda a: pl.BlockSpec(a.shape, lambda i, g: (0,) * a.ndim)
    return pl.pallas_call(
        _moe_kernel,
        grid=(T // tt, N_GROUPS),
        in_specs=[pl.BlockSpec((tt, D), lambda i, g: (i, 0)), full(gm), full(wr), full(br),
                  pl.BlockSpec((1, EXPERTS_PER_GROUP, D, EXPERT_FF), lambda i, g: (g, 0, 0, 0)),
                  pl.BlockSpec((1, EXPERTS_PER_GROUP, D, EXPERT_FF), lambda i, g: (g, 0, 0, 0)),
                  pl.BlockSpec((1, EXPERTS_PER_GROUP, EXPERT_FF, D), lambda i, g: (g, 0, 0, 0)),
                  full(gf)],
        out_specs=pl.BlockSpec((tt, D), lambda i, g: (i, 0)),
        out_shape=jax.ShapeDtypeStruct((T, D), F32),
        scratch_shapes=[pltpu.VMEM((tt, D), BF16), pltpu.VMEM((tt, LANES), F32),
                        pltpu.VMEM((tt, 1), I32),
                        pltpu.SMEM((N_GROUPS + 1,), I32)],
        compiler_params=_cparams(("parallel", "arbitrary")),
        name="moe",
    )(x2d, gm, wr, br, wg, wu, wd, gf)


def kernel(x, mem, positions, norm_mix_g, w_in, w_o_attn, conv_w, conv_b, conv_ln_g, conv_ln_b,
           w_conv_out, w_out, norm_x_g, norm_mem_g, w_q_x, w_kv_x, w_o_x, norm_moe_g,
           w_router_group, b_router_group, w_router_expert, b_router_expert,
           w_exp_gate, w_exp_up, w_exp_down, norm_final_g):
    B, S, D = x.shape
    T = B * S
    depth = norm_mix_g.shape[0]

    inv_freq = ROPE_THETA ** (-jnp.arange(0, HEAD_DIM, 2, dtype=F32) / HEAD_DIM)
    ang = positions.astype(F32)[..., None] * inv_freq
    cos, sin = jnp.cos(ang), jnp.sin(ang)
    cos128 = jnp.tile(cos, (1, 1, 4)).reshape(T, LANES)
    sin128 = jnp.tile(jnp.concatenate([-sin, sin], axis=-1), (1, 1, 2)).reshape(T, LANES)

    row = lambda a: a.reshape(1, -1)
    for l in range(depth):
        w = w_in[l]
        wm = jnp.concatenate([w[:, 0:512], w[:, 512:1024], w[:, 1536:2048],
                              w[:, 2120:3144], w[:, 3144:5192]], axis=1).astype(BF16)
        ws = jnp.pad(w[:, 2048:2112], ((0, 0), (0, LANES - IDX_DIM))).astype(BF16)
        wvt = w[:, 1024:1536].T.astype(BF16)
        wwt = w[:, 2112:2120].T.astype(BF16)
        q, k, qi, vt_blk, ki, wi_t, uc, gs = _in_proj(
            x.reshape(T, D), row(norm_mix_g[l]), wm, ws, wvt, wwt, cos128, sin128,
            jnp.pad(conv_w[l], ((0, 1), (0, 0))), row(conv_b[l]), row(conv_ln_g[l]), row(conv_ln_b[l]), S)
        attn = _dsa(q, qi, wi_t, ki, k, vt_blk, B, S).reshape(B, S, ATTN_WIDTH)
        uc = uc.reshape(B, S, CONV_CH)
        kv = _mem_kv(mem, row(norm_mem_g[l]), w_kv_x[l].astype(BF16))
        x = _mix_xattn(x, attn, uc, gs.reshape(B, S, 2 * D), kv,
                       w_o_attn[l].astype(BF16), w_conv_out[l].astype(BF16), w_out[l].astype(BF16),
                       row(norm_x_g[l]), w_q_x[l].astype(BF16), w_o_x[l].astype(BF16))

        wr = jnp.concatenate([w_router_group[l],
                              w_router_expert[l].transpose(1, 0, 2).reshape(D, N_EXPERTS)], axis=1)
        wr = jnp.pad(wr, ((0, 0), (0, LANES - N_GROUPS - N_EXPERTS))).astype(BF16)
        br = jnp.pad(jnp.concatenate([b_router_group[l], b_router_expert[l].reshape(-1)]),
                     (0, LANES - N_GROUPS - N_EXPERTS)).reshape(1, LANES)
        last = l == depth - 1
        gf = row(norm_final_g) if last else None
        assert last, "the final norm is fused into the last layer's MoE kernel"
        x = _moe(x.reshape(T, D), row(norm_moe_g[l]), wr, br, w_exp_gate[l].astype(BF16),
                 w_exp_up[l].astype(BF16), w_exp_down[l].astype(BF16), gf).reshape(B, S, D)
    return x
```
